```python
import math
import jax, jax.numpy as jnp
from jax import lax
import numpy as np

D_MODEL = 1024
BATCH = 4
SEQ = 8192
DEPTH = 2
DEC_BATCH = 16
DEC_SEQ = 4096
PAST_LEN = 128

N_FOURIER_GROUPS = 4
FOURIER_GROUP = 64
FOURIER_WIDTH = N_FOURIER_GROUPS * FOURIER_GROUP
SGU_HEADS = 4
SGU_HEAD_DIM = 64
SGU_WIDTH = SGU_HEADS * SGU_HEAD_DIM
CHUNK = 128
DIFF_HEADS = 4
DIFF_HEAD_DIM = 64
DIFF_V_DIM = 2 * DIFF_HEAD_DIM
DIFF_QK_WIDTH = DIFF_HEADS * 2 * DIFF_HEAD_DIM
DIFF_V_WIDTH = DIFF_HEADS * DIFF_V_DIM
Q_BLOCK = 128
ROPE_THETA = 10000.0
N_BRANCHES = 3
SPLITS = (FOURIER_WIDTH,
          FOURIER_WIDTH + SGU_WIDTH,
          FOURIER_WIDTH + 2 * SGU_WIDTH,
          FOURIER_WIDTH + 2 * SGU_WIDTH + DIFF_QK_WIDTH,
          FOURIER_WIDTH + 2 * SGU_WIDTH + 2 * DIFF_QK_WIDTH,
          FOURIER_WIDTH + 2 * SGU_WIDTH + 2 * DIFF_QK_WIDTH + DIFF_V_WIDTH)
IN_WIDTH = SPLITS[-1] + N_BRANCHES * D_MODEL
D_FF = 2752
N_EXPERTS = 8
TOP_K = 2
N_DENSE = (DEPTH + 1) // 2
N_MOE = DEPTH // 2
ALPHA = (2 * DEPTH) ** 0.25
BETA = (8 * DEPTH) ** -0.25
LN_EPS = 1e-5
RMS_EPS = 1e-5

kernel_name = "fourier_sgu_diffattn_gated_deepnorm_encoder"


def layer_norm(x, g, b):
    xf = x.astype(jnp.float32)
    mu = jnp.mean(xf, -1, keepdims=True)
    var = jnp.mean(jnp.square(xf - mu), -1, keepdims=True)
    return ((xf - mu) * lax.rsqrt(var + LN_EPS) * g.astype(jnp.float32) + b.astype(jnp.float32)).astype(x.dtype)


def rope_tables(seq):
    inv = ROPE_THETA ** (-jnp.arange(0, DIFF_HEAD_DIM, 2, dtype=jnp.float32) / DIFF_HEAD_DIM)
    ang = jnp.arange(seq, dtype=jnp.float32)[:, None] * inv[None, :]
    return jnp.cos(ang), jnp.sin(ang)


def apply_rope(t, cos, sin):
    t1, t2 = jnp.split(t.astype(jnp.float32), 2, axis=-1)
    c = cos[None, :, None, None, :]
    s = sin[None, :, None, None, :]
    return jnp.concatenate([t1 * c - t2 * s, t2 * c + t1 * s], axis=-1).astype(t.dtype)


def fourier_mix(f):
    B, S, _ = f.shape
    fr = f.astype(jnp.float32).reshape(B, S, N_FOURIER_GROUPS, FOURIER_GROUP)
    out = jnp.fft.fft2(fr, axes=(1, 3), norm="ortho").real
    return out.reshape(B, S, FOURIER_WIDTH).astype(f.dtype)


def spatial_gating(u, v, vn_g, vn_b, sgu_w, sgu_b):
    B, S, _ = v.shape
    vn = layer_norm(v, vn_g, vn_b)
    vr = vn.reshape(B, S // CHUNK, CHUNK, SGU_HEADS, SGU_HEAD_DIM)
    mixed = jnp.einsum('hij,bcjhd->bcihd', sgu_w, vr) + jnp.transpose(sgu_b)[None, None, :, :, None]
    return u * mixed.reshape(B, S, SGU_WIDTH)


def diff_attention(q, k, v, lam, subln_g, lambda_init):
    B, S = q.shape[:2]
    nb = S // Q_BLOCK
    qb = jnp.moveaxis(q.reshape(B, nb, Q_BLOCK, DIFF_HEADS, 2, DIFF_HEAD_DIM), 1, 0)
    scale = DIFF_HEAD_DIM ** -0.5

    def block(qblk):
        s = jnp.einsum('bqhmd,bkhmd->bhmqk', qblk, k).astype(jnp.float32) * scale
        p = jax.nn.softmax(s, axis=-1)
        a = p[:, :, 0] - lam * p[:, :, 1]
        return jnp.einsum('bhqk,bkhe->bqhe', a.astype(v.dtype), v)

    o = lax.map(block, qb)
    o = jnp.moveaxis(o, 0, 1).reshape(B, S, DIFF_HEADS, DIFF_V_DIM).astype(jnp.float32)
    o = o * lax.rsqrt(jnp.mean(o * o, -1, keepdims=True) + RMS_EPS) * subln_g.astype(jnp.float32)
    o = o * (1.0 - lambda_init)
    return o.reshape(B, S, DIFF_V_WIDTH).astype(v.dtype)


def swiglu(x, w_gate, w_up, w_down):
    return (jax.nn.silu(x @ w_gate) * (x @ w_up)) @ w_down


def moe_swiglu(x, w_router, w_gate, w_up, w_down):
    B, S, D = x.shape
    xt = x.reshape(-1, D)
    logits = (xt @ w_router).astype(jnp.float32)
    top_v, top_i = lax.top_k(logits, TOP_K)
    top_w = jax.nn.softmax(top_v, axis=-1)
    combine = jnp.sum(jax.nn.one_hot(top_i, N_EXPERTS, dtype=jnp.float32) * top_w[..., None], axis=1)
    y = jnp.zeros_like(xt)
    for e in range(N_EXPERTS):
        y = y + combine[:, e:e + 1].astype(xt.dtype) * swiglu(xt, w_gate[e], w_up[e], w_down[e])
    return y.reshape(B, S, D)


def trunk(x, w_in, w_fourier, w_sgu, w_diff, w_out, vn_g, vn_b, sgu_w, sgu_b,
          lam_q1, lam_k1, lam_q2, lam_k2, subln_g, ln1_g, ln1_b, ln2_g, ln2_b,
          ffn_w_gate, ffn_w_up, ffn_w_down, w_router, moe_w_gate, moe_w_up, moe_w_down):
    B, S, D = x.shape
    cos, sin = rope_tables(S)
    for l in range(DEPTH):
        h = x @ w_in[l]
        f_in, u, v, q, k, va, g = jnp.split(h, SPLITS, axis=-1)
        fo = fourier_mix(f_in)
        so = spatial_gating(u, v, vn_g[l], vn_b[l], sgu_w[l], sgu_b[l])
        q = apply_rope(q.reshape(B, S, DIFF_HEADS, 2, DIFF_HEAD_DIM), cos, sin)
        k = apply_rope(k.reshape(B, S, DIFF_HEADS, 2, DIFF_HEAD_DIM), cos, sin)
        va = va.reshape(B, S, DIFF_HEADS, DIFF_V_DIM)
        lambda_init = 0.8 - 0.6 * math.exp(-0.3 * l)
        lam = (jnp.exp(jnp.sum(lam_q1[l].astype(jnp.float32) * lam_k1[l].astype(jnp.float32)))
               - jnp.exp(jnp.sum(lam_q2[l].astype(jnp.float32) * lam_k2[l].astype(jnp.float32)))
               + lambda_init)
        do = diff_attention(q, k, va, lam, subln_g[l], lambda_init)
        gates = jax.nn.sigmoid(g.reshape(B, S, N_BRANCHES, D))
        merged = (gates[:, :, 0] * (fo @ w_fourier[l])
                  + gates[:, :, 1] * (so @ w_sgu[l])
                  + gates[:, :, 2] * (do @ w_diff[l]))
        x = layer_norm(ALPHA * x + merged @ w_out[l], ln1_g[l], ln1_b[l])
        if l % 2 == 0:
            j = l // 2
            ff = swiglu(x, ffn_w_gate[j], ffn_w_up[j], ffn_w_down[j])
        else:
            j = l // 2
            ff = moe_swiglu(x, w_router[j], moe_w_gate[j], moe_w_up[j], moe_w_down[j])
        x = layer_norm(ALPHA * x + ff, ln2_g[l], ln2_b[l])
    return x


def setup_inputs(seed: int = 0) -> dict:
    key = jax.random.key(seed)
    ks = jax.random.split(key, 32)
    f32 = jnp.float32
    nrm = lambda k, shape, scale: jax.random.normal(k, shape, f32) * scale
    return {
        "x_prompt": jax.random.normal(ks[0], (BATCH, SEQ, D_MODEL), f32),
        "x_sample": jax.random.normal(ks[1], (DEC_BATCH, DEC_SEQ, D_MODEL), f32),
        "w_in": nrm(ks[2], (DEPTH, D_MODEL, IN_WIDTH), D_MODEL ** -0.5),
        "w_fourier": nrm(ks[3], (DEPTH, FOURIER_WIDTH, D_MODEL), BETA * FOURIER_WIDTH ** -0.5),
        "w_sgu": nrm(ks[4], (DEPTH, SGU_WIDTH, D_MODEL), BETA * SGU_WIDTH ** -0.5),
        "w_diff": nrm(ks[5], (DEPTH, DIFF_V_WIDTH, D_MODEL), BETA * DIFF_V_WIDTH ** -0.5),
        "w_out": nrm(ks[6], (DEPTH, D_MODEL, D_MODEL), BETA * D_MODEL ** -0.5),
        "vn_g": 1.0 + nrm(ks[7], (DEPTH, SGU_WIDTH), 0.02),
        "vn_b": nrm(ks[8], (DEPTH, SGU_WIDTH), 0.02),
        "sgu_w": nrm(ks[9], (DEPTH, SGU_HEADS, CHUNK, CHUNK), 0.5 * CHUNK ** -0.5),
        "sgu_b": 1.0 + nrm(ks[10], (DEPTH, SGU_HEADS, CHUNK), 0.01),
        "lam_q1": nrm(ks[11], (DEPTH, DIFF_HEAD_DIM), 0.1),
        "lam_k1": nrm(ks[12], (DEPTH, DIFF_HEAD_DIM), 0.1),
        "lam_q2": nrm(ks[13], (DEPTH, DIFF_HEAD_DIM), 0.1),
        "lam_k2": nrm(ks[14], (DEPTH, DIFF_HEAD_DIM), 0.1),
        "subln_g": 1.0 + nrm(ks[15], (DEPTH, DIFF_V_DIM), 0.02),
        "ln1_g": 1.0 + nrm(ks[16], (DEPTH, D_MODEL), 0.02),
        "ln1_b": nrm(ks[17], (DEPTH, D_MODEL), 0.02),
        "ln2_g": 1.0 + nrm(ks[18], (DEPTH, D_MODEL), 0.02),
        "ln2_b": nrm(ks[19], (DEPTH, D_MODEL), 0.02),
        "ffn_w_gate": nrm(ks[20], (N_DENSE, D_MODEL, D_FF), BETA * D_MODEL ** -0.5),
        "ffn_w_up": nrm(ks[21], (N_DENSE, D_MODEL, D_FF), BETA * D_MODEL ** -0.5),
        "ffn_w_down": nrm(ks[22], (N_DENSE, D_FF, D_MODEL), BETA * D_FF ** -0.5),
        "w_router": nrm(ks[23], (N_MOE, D_MODEL, N_EXPERTS), D_MODEL ** -0.5),
        "moe_w_gate": nrm(ks[24], (N_MOE, N_EXPERTS, D_MODEL, D_FF), BETA * D_MODEL ** -0.5),
        "moe_w_up": nrm(ks[25], (N_MOE, N_EXPERTS, D_MODEL, D_FF), BETA * D_MODEL ** -0.5),
        "moe_w_down": nrm(ks[26], (N_MOE, N_EXPERTS, D_FF, D_MODEL), BETA * D_FF ** -0.5),
    }


def reference(x_prompt, x_sample, w_in, w_fourier, w_sgu, w_diff, w_out, vn_g, vn_b, sgu_w, sgu_b,
              lam_q1, lam_k1, lam_q2, lam_k2, subln_g, ln1_g, ln1_b, ln2_g, ln2_b,
              ffn_w_gate, ffn_w_up, ffn_w_down, w_router, moe_w_gate, moe_w_up, moe_w_down):
    y_prompt = trunk(x_prompt, w_in, w_fourier, w_sgu, w_diff, w_out, vn_g, vn_b, sgu_w, sgu_b,
                     lam_q1, lam_k1, lam_q2, lam_k2, subln_g, ln1_g, ln1_b, ln2_g, ln2_b,
                     ffn_w_gate, ffn_w_up, ffn_w_down, w_router, moe_w_gate, moe_w_up, moe_w_down)
    y_sample = trunk(x_sample, w_in, w_fourier, w_sgu, w_diff, w_out, vn_g, vn_b, sgu_w, sgu_b,
                     lam_q1, lam_k1, lam_q2, lam_k2, subln_g, ln1_g, ln1_b, ln2_g, ln2_b,
                     ffn_w_gate, ffn_w_up, ffn_w_down, w_router, moe_w_gate, moe_w_up, moe_w_down)
    return (y_prompt, y_sample)
```

```python
import functools
import math

import jax
import jax.numpy as jnp
from jax import lax
from jax.experimental import pallas as pl
from jax.experimental.pallas import tpu as pltpu

F32 = jnp.float32
BF16 = jnp.bfloat16

D_MODEL = 1024
DEPTH = 2
N_FOURIER_GROUPS = 4
FOURIER_GROUP = 64
FOURIER_WIDTH = N_FOURIER_GROUPS * FOURIER_GROUP
SGU_HEADS = 4
SGU_HEAD_DIM = 64
SGU_WIDTH = SGU_HEADS * SGU_HEAD_DIM
CHUNK = 128
DIFF_HEADS = 4
DIFF_HEAD_DIM = 64
DIFF_V_DIM = 2 * DIFF_HEAD_DIM
DIFF_QK_WIDTH = DIFF_HEADS * 2 * DIFF_HEAD_DIM
DIFF_V_WIDTH = DIFF_HEADS * DIFF_V_DIM
ROPE_THETA = 10000.0
N_BRANCHES = 3
D_FF = 2752
N_EXPERTS = 8
ALPHA = (2 * DEPTH) ** 0.25
LN_EPS = 1e-5
RMS_EPS = 1e-5

OFF_F = 0
OFF_U = OFF_F + FOURIER_WIDTH
OFF_V = OFF_U + SGU_WIDTH
OFF_Q = OFF_V + SGU_WIDTH
OFF_K = OFF_Q + DIFF_QK_WIDTH
OFF_VA = OFF_K + DIFF_QK_WIDTH
OFF_G = OFF_VA + DIFF_V_WIDTH
IN_WIDTH = OFF_G + N_BRANCHES * D_MODEL

LANES = 128
D_FF_PAD = 2816
ROUTER_PAD = LANES
MIB = 1024 * 1024


def _tiles(S):
    t = dict(
        tm_in=min(512, S),
        tm_dft=min(256, S),
        tq=min(max((2 * MIB) // S, 128), S),
        tk=min(1024, S),
        tf=D_FF_PAD // 2,
    )
    return t


def _layer_norm(z, g, b):
    mu = jnp.mean(z, axis=-1, keepdims=True)
    zc = z - mu
    var = jnp.mean(zc * zc, axis=-1, keepdims=True)
    return zc * lax.rsqrt(var + LN_EPS) * g + b


def _dot(a, b):
    return jnp.dot(a, b, preferred_element_type=F32)


def _inproj_kernel(x_ref, w_ref, dft_ref, vng_ref, vnb_ref, sguw_ref, sgub_ref, cos_ref, sin_ref,
                   g_out, so_out, q_out, kt_out, va_out):
    tm = x_ref.shape[0]
    xb = x_ref[...].astype(BF16)

    def proj(a, b):
        return _dot(xb, w_ref[:, a:b])

    f = proj(OFF_F, OFF_U)
    g_out[...] = _dot(f.astype(BF16), dft_ref[...]).astype(BF16)

    u = proj(OFF_U, OFF_V)
    v = proj(OFF_V, OFF_Q)
    vn = _layer_norm(v, vng_ref[...], vnb_ref[...]).astype(BF16)
    row_head = lax.broadcasted_iota(jnp.int32, (SGU_HEADS * CHUNK, SGU_WIDTH), 0) // CHUNK
    col_head = lax.broadcasted_iota(jnp.int32, (SGU_HEADS * CHUNK, SGU_WIDTH), 1) // SGU_HEAD_DIM
    head_mask = row_head == col_head
    for c in range(tm // CHUNK):
        rows = slice(c * CHUNK, (c + 1) * CHUNK)
        vb = vn[rows, :]
        rhs = jnp.where(head_mask, jnp.concatenate([vb] * SGU_HEADS, axis=0), jnp.zeros((), BF16))
        mixed = _dot(sguw_ref[...], rhs) + sgub_ref[...]
        so_out[rows, :] = (u[rows, :] * mixed).astype(BF16)

    cs = cos_ref[...]
    sn = sin_ref[...]
    lane = lax.broadcasted_iota(jnp.int32, (tm, LANES), 1)
    first_half = (lane % DIFF_HEAD_DIM) < (DIFF_HEAD_DIM // 2)

    def rope(t):
        outs = []
        for j in range(DIFF_QK_WIDTH // LANES):
            tc = t[:, j * LANES:(j + 1) * LANES]
            rot = jnp.where(first_half,
                            pltpu.roll(tc, LANES - DIFF_HEAD_DIM // 2, 1),
                            pltpu.roll(tc, DIFF_HEAD_DIM // 2, 1))
            outs.append(tc * cs + rot * sn)
        return jnp.concatenate(outs, axis=1)

    q = rope(proj(OFF_Q, OFF_K)) * (DIFF_HEAD_DIM ** -0.5)
    q_out[...] = q.astype(BF16)
    k = rope(proj(OFF_K, OFF_VA))
    kt_out[...] = k.T.astype(BF16)
    va_out[...] = proj(OFF_VA, OFF_G).astype(BF16)


def _inproj(x, w_a, dft64, vn_g, vn_b, sgu_wcat, sgu_bias, rope_cos, rope_sin, S):
    T = x.shape[0]
    tm = _tiles(S)["tm_in"]
    n_pos = S // tm
    full = lambda shape: pl.BlockSpec(shape, lambda i: (0,) * len(shape))
    return pl.pallas_call(
        _inproj_kernel,
        grid=(T // tm,),
        in_specs=[
            pl.BlockSpec((tm, D_MODEL), lambda i: (i, 0)),
            full((D_MODEL, OFF_G)),
            full((FOURIER_WIDTH, 2 * FOURIER_WIDTH)),
            full((1, SGU_WIDTH)),
            full((1, SGU_WIDTH)),
            full((CHUNK, SGU_HEADS * CHUNK)),
            full((CHUNK, SGU_WIDTH)),
            pl.BlockSpec((tm, LANES), lambda i: (i % n_pos, 0)),
            pl.BlockSpec((tm, LANES), lambda i: (i % n_pos, 0)),
        ],
        out_specs=[
            pl.BlockSpec((tm, 2 * FOURIER_WIDTH), lambda i: (i, 0)),
            pl.BlockSpec((tm, SGU_WIDTH), lambda i: (i, 0)),
            pl.BlockSpec((tm, DIFF_QK_WIDTH), lambda i: (i, 0)),
            pl.BlockSpec((DIFF_QK_WIDTH, tm), lambda i: (0, i)),
            pl.BlockSpec((tm, DIFF_V_WIDTH), lambda i: (i, 0)),
        ],
        out_shape=[
            jax.ShapeDtypeStruct((T, 2 * FOURIER_WIDTH), BF16),
            jax.ShapeDtypeStruct((T, SGU_WIDTH), BF16),
            jax.ShapeDtypeStruct((T, DIFF_QK_WIDTH), BF16),
            jax.ShapeDtypeStruct((DIFF_QK_WIDTH, T), BF16),
            jax.ShapeDtypeStruct((T, DIFF_V_WIDTH), BF16),
        ],
        compiler_params=pltpu.CompilerParams(
            dimension_semantics=("parallel",), vmem_limit_bytes=48 * MIB),
        name="inproj",
    )(x, w_a, dft64, vn_g, vn_b, sgu_wcat, sgu_bias, rope_cos, rope_sin)


def _fourier_kernel(ct_ref, snt_ref, g_ref, o_ref):
    acc = _dot(ct_ref[...], g_ref[:, :FOURIER_WIDTH])
    acc = acc + _dot(snt_ref[...], g_ref[:, FOURIER_WIDTH:])
    o_ref[...] = acc.astype(BF16)


def _fourier(g, ct, snt, B, S):
    tm = _tiles(S)["tm_dft"]
    n_row = S // tm
    return pl.pallas_call(
        _fourier_kernel,
        grid=(n_row, B),
        in_specs=[
            pl.BlockSpec((tm, S), lambda i, b: (i, 0)),
            pl.BlockSpec((tm, S), lambda i, b: (i, 0)),
            pl.BlockSpec((S, 2 * FOURIER_WIDTH), lambda i, b: (b, 0)),
        ],
        out_specs=pl.BlockSpec((tm, FOURIER_WIDTH), lambda i, b: (b * n_row + i, 0)),
        out_shape=jax.ShapeDtypeStruct((B * S, FOURIER_WIDTH), BF16),
        compiler_params=pltpu.CompilerParams(
            dimension_semantics=("parallel", "parallel"), vmem_limit_bytes=48 * MIB),
        name="fourier",
    )(ct, snt, g)


def _attn_kernel(lam_ref, q_ref, kt_ref, v_ref, g_ref, o_ref, e1_ref, e2_ref, *, tk, lambda_init):
    tq = q_ref.shape[0]
    S = kt_ref.shape[1]
    n_chunks = S // tk
    lp = lam_ref[...]
    lam = (jnp.exp(jnp.sum(lp[0:1, :] * lp[1:2, :], axis=-1, keepdims=True))
           - jnp.exp(jnp.sum(lp[2:3, :] * lp[3:4, :], axis=-1, keepdims=True))
           + lambda_init)
    q = q_ref[...]

    def scores_pass(qm, row0, e_ref):
        m = jnp.full((tq, 1), -jnp.inf, F32)
        l = jnp.zeros((tq, 1), F32)
        used = []
        for c in range(n_chunks):
            cols = slice(c * tk, (c + 1) * tk)
            s = _dot(qm, kt_ref[row0:row0 + DIFF_HEAD_DIM, cols])
            m_new = jnp.maximum(m, jnp.max(s, axis=-1, keepdims=True))
            e = jnp.exp(s - m_new)
            l = l * jnp.exp(m - m_new) + jnp.sum(e, axis=-1, keepdims=True)
            e_ref[:, cols] = e
            used.append(m_new)
            m = m_new
        return m, l, used

    m1, l1, used1 = scores_pass(q[:, :DIFF_HEAD_DIM], 0, e1_ref)
    m2, l2, used2 = scores_pass(q[:, DIFF_HEAD_DIM:], DIFF_HEAD_DIM, e2_ref)

    acc = jnp.zeros((tq, DIFF_V_DIM), F32)
    for c in range(n_chunks):
        cols = slice(c * tk, (c + 1) * tk)
        f1 = jnp.exp(used1[c] - m1) / l1
        f2 = lam * jnp.exp(used2[c] - m2) / l2
        a = e1_ref[:, cols] * f1 - e2_ref[:, cols] * f2
        acc = acc + _dot(a.astype(BF16), v_ref[cols, :])

    o = acc * lax.rsqrt(jnp.mean(acc * acc, axis=-1, keepdims=True) + RMS_EPS) * g_ref[...]
    o_ref[...] = (o * (1.0 - lambda_init)).astype(BF16)


def _attn(lam_params, q, kt, va, subln_g, B, S, lambda_init):
    t = _tiles(S)
    tq, tk = t["tq"], t["tk"]
    n_q = S // tq
    kern = functools.partial(_attn_kernel, tk=tk, lambda_init=lambda_init)
    return pl.pallas_call(
        kern,
        grid=(B, DIFF_HEADS, n_q),
        in_specs=[
            pl.BlockSpec((4, DIFF_HEAD_DIM), lambda b, h, i: (0, 0)),
            pl.BlockSpec((tq, DIFF_V_DIM), lambda b, h, i: (b * n_q + i, h)),
            pl.BlockSpec((DIFF_V_DIM, S), lambda b, h, i: (h, b)),
            pl.BlockSpec((S, DIFF_V_DIM), lambda b, h, i: (b, h)),
            pl.BlockSpec((1, DIFF_V_DIM), lambda b, h, i: (0, 0)),
        ],
        out_specs=pl.BlockSpec((tq, DIFF_V_DIM), lambda b, h, i: (b * n_q + i, h)),
        out_shape=jax.ShapeDtypeStruct((B * S, DIFF_V_WIDTH), BF16),
        scratch_shapes=[pltpu.VMEM((tq, S), F32), pltpu.VMEM((tq, S), F32)],
        compiler_params=pltpu.CompilerParams(
            dimension_semantics=("parallel", "parallel", "parallel"), vmem_limit_bytes=48 * MIB),
        name="attn",
    )(lam_params, q, kt, va, subln_g)


def _merge_kernel(*refs, with_router):
    if with_router:
        (x_ref, fo_ref, so_ref, do_ref, wg_ref, wf_ref, ws_ref, wd_ref, wo_ref, lg_ref, lb_ref,
         wr_ref, x1_out, cw_out) = refs
    else:
        (x_ref, fo_ref, so_ref, do_ref, wg_ref, wf_ref, ws_ref, wd_ref, wo_ref, lg_ref, lb_ref,
         x1_out) = refs
    x = x_ref[...]
    xb = x.astype(BF16)
    merged = None
    for j, (br_ref, wb_ref) in enumerate(((fo_ref, wf_ref), (so_ref, ws_ref), (do_ref, wd_ref))):
        gate = jax.nn.sigmoid(_dot(xb, wg_ref[:, j * D_MODEL:(j + 1) * D_MODEL]))
        term = gate * _dot(br_ref[...], wb_ref[...])
        merged = term if merged is None else merged + term
    y = _dot(merged.astype(BF16), wo_ref[...])
    x1 = _layer_norm(ALPHA * x + y, lg_ref[...], lb_ref[...])
    x1_out[...] = x1

    if with_router:
        tm = x.shape[0]
        logits = jnp.dot(x1, wr_ref[...], preferred_element_type=F32, precision=lax.Precision.HIGHEST)
        lane = lax.broadcasted_iota(jnp.int32, (tm, ROUTER_PAD), 1)
        neg = jnp.float32(-jnp.inf)
        lg = jnp.where(lane < N_EXPERTS, logits, neg)
        v1 = jnp.max(lg, axis=-1, keepdims=True)
        i1 = jnp.min(jnp.where(lg == v1, lane, ROUTER_PAD), axis=-1, keepdims=True)
        lg2 = jnp.where(lane == i1, neg, lg)
        v2 = jnp.max(lg2, axis=-1, keepdims=True)
        i2 = jnp.min(jnp.where(lg2 == v2, lane, ROUTER_PAD), axis=-1, keepdims=True)
        e2 = jnp.exp(v2 - v1)
        den = 1.0 + e2
        cw_out[...] = jnp.where(lane == i1, 1.0 / den, 0.0) + jnp.where(lane == i2, e2 / den, 0.0)


def _merge(x, fo, so, do, w_g, w_f, w_s, w_d, w_o, ln_g, ln_b, w_router, S):
    T = x.shape[0]
    tm = _tiles(S)["tm_in"]
    with_router = w_router is not None
    full = lambda shape: pl.BlockSpec(shape, lambda i: (0,) * len(shape))
    row = lambda width: pl.BlockSpec((tm, width), lambda i: (i, 0))
    in_specs = [
        row(D_MODEL), row(FOURIER_WIDTH), row(SGU_WIDTH), row(DIFF_V_WIDTH),
        full((D_MODEL, N_BRANCHES * D_MODEL)),
        full((FOURIER_WIDTH, D_MODEL)), full((SGU_WIDTH, D_MODEL)), full((DIFF_V_WIDTH, D_MODEL)),
        full((D_MODEL, D_MODEL)), full((1, D_MODEL)), full((1, D_MODEL)),
    ]
    args = [x, fo, so, do, w_g, w_f, w_s, w_d, w_o, ln_g, ln_b]
    out_specs = [row(D_MODEL)]
    out_shape = [jax.ShapeDtypeStruct((T, D_MODEL), F32)]
    if with_router:
        in_specs.append(full((D_MODEL, ROUTER_PAD)))
        args.append(w_router)
        out_specs.append(row(ROUTER_PAD))
        out_shape.append(jax.ShapeDtypeStruct((T, ROUTER_PAD), F32))
    outs = pl.pallas_call(
        functools.partial(_merge_kernel, with_router=with_router),
        grid=(T // tm,),
        in_specs=in_specs,
        out_specs=out_specs,
        out_shape=out_shape,
        compiler_params=pltpu.CompilerParams(
            dimension_semantics=("parallel",), vmem_limit_bytes=56 * MIB),
        name="merge_router" if with_router else "merge",
    )(*args)
    return (outs[0], outs[1]) if with_router else (outs[0], None)


def _ffn_kernel(*refs, with_router):
    if with_router:
        x_ref, cw_ref, wg_ref, wu_ref, wd_ref, lg_ref, lb_ref, o_ref, xb_ref, acc_ref = refs
    else:
        x_ref, wg_ref, wu_ref, wd_ref, lg_ref, lb_ref, o_ref, xb_ref, acc_ref = refs
    e = pl.program_id(1)
    f = pl.program_id(2)
    first = jnp.logical_and(e == 0, f == 0)
    last = jnp.logical_and(e == pl.num_programs(1) - 1, f == pl.num_programs(2) - 1)

    @pl.when(first)
    def _():
        xb_ref[...] = x_ref[...].astype(BF16)
        acc_ref[...] = jnp.zeros_like(acc_ref)

    xb = xb_ref[...]
    h = jax.nn.silu(_dot(xb, wg_ref[0])) * _dot(xb, wu_ref[0])
    if with_router:
        lane = lax.broadcasted_iota(jnp.int32, cw_ref.shape, 1)
        h = h * jnp.sum(jnp.where(lane == e, cw_ref[...], 0.0), axis=-1, keepdims=True)
    acc_ref[...] += _dot(h.astype(BF16), wd_ref[0])

    @pl.when(last)
    def _():
        o_ref[...] = _layer_norm(ALPHA * x_ref[...] + acc_ref[...], lg_ref[...], lb_ref[...])


def _ffn(x, cw, w_gate, w_up, w_down, ln_g, ln_b, S):
    T = x.shape[0]
    t = _tiles(S)
    tm, tf = t["tm_in"], t["tf"]
    n_e = w_gate.shape[0]
    n_f = D_FF_PAD // tf
    with_router = cw is not None
    in_specs = [pl.BlockSpec((tm, D_MODEL), lambda i, e, f: (i, 0))]
    args = [x]
    if with_router:
        in_specs.append(pl.BlockSpec((tm, ROUTER_PAD), lambda i, e, f: (i, 0)))
        args.append(cw)
    in_specs += [
        pl.BlockSpec((1, D_MODEL, tf), lambda i, e, f: (e, 0, f)),
        pl.BlockSpec((1, D_MODEL, tf), lambda i, e, f: (e, 0, f)),
        pl.BlockSpec((1, tf, D_MODEL), lambda i, e, f: (e, f, 0)),
        pl.BlockSpec((1, D_MODEL), lambda i, e, f: (0, 0)),
        pl.BlockSpec((1, D_MODEL), lambda i, e, f: (0, 0)),
    ]
    args += [w_gate, w_up, w_down, ln_g, ln_b]
    return pl.pallas_call(
        functools.partial(_ffn_kernel, with_router=with_router),
        grid=(T // tm, n_e, n_f),
        in_specs=in_specs,
        out_specs=pl.BlockSpec((tm, D_MODEL), lambda i, e, f: (i, 0)),
        out_shape=jax.ShapeDtypeStruct((T, D_MODEL), F32),
        scratch_shapes=[pltpu.VMEM((tm, D_MODEL), BF16), pltpu.VMEM((tm, D_MODEL), F32)],
        compiler_params=pltpu.CompilerParams(
            dimension_semantics=("parallel", "arbitrary", "arbitrary"), vmem_limit_bytes=56 * MIB),
        name="ffn_moe" if with_router else "ffn_dense",
    )(*args)


def _rope_tables(S):
    half = DIFF_HEAD_DIM // 2
    inv = ROPE_THETA ** (-jnp.arange(0, DIFF_HEAD_DIM, 2, dtype=F32) / DIFF_HEAD_DIM)
    ang = jnp.arange(S, dtype=F32)[:, None] * inv[None, :]
    cos, sin = jnp.cos(ang), jnp.sin(ang)
    reps = LANES // DIFF_HEAD_DIM
    cos_t = jnp.tile(jnp.concatenate([cos, cos], axis=1), (1, reps))
    sin_t = jnp.tile(jnp.concatenate([-sin, sin], axis=1), (1, reps))
    assert cos_t.shape == (S, LANES) and half * 2 == DIFF_HEAD_DIM
    return cos_t, sin_t


def _dft_tables(S):
    j = jnp.arange(S, dtype=jnp.int32)
    ang = ((j[:, None] * j[None, :]) % S).astype(F32) * (2.0 * math.pi / S)
    scale = S ** -0.5
    return (jnp.cos(ang) * scale).astype(BF16), (jnp.sin(ang) * -scale).astype(BF16)


def _dft64_table():
    j = jnp.arange(FOURIER_GROUP, dtype=jnp.int32)
    ang = ((j[:, None] * j[None, :]) % FOURIER_GROUP).astype(F32) * (2.0 * math.pi / FOURIER_GROUP)
    scale = FOURIER_GROUP ** -0.5
    eye = jnp.eye(N_FOURIER_GROUPS, dtype=F32)
    return jnp.concatenate([jnp.kron(eye, jnp.cos(ang) * scale),
                            jnp.kron(eye, jnp.sin(ang) * scale)], axis=1).astype(BF16)


def _pad_ff(w, axis):
    pad = [(0, 0)] * w.ndim
    pad[axis] = (0, D_FF_PAD - D_FF)
    return jnp.pad(w, pad).astype(BF16)


def _prepare(w_in, w_fourier, w_sgu, w_diff, w_out, vn_g, vn_b, sgu_w, sgu_b,
             lam_q1, lam_k1, lam_q2, lam_k2, subln_g, ln1_g, ln1_b, ln2_g, ln2_b,
             ffn_w_gate, ffn_w_up, ffn_w_down, w_router, moe_w_gate, moe_w_up, moe_w_down):
    w_in_b = w_in.astype(BF16)
    row = lambda a: a.astype(F32)[:, None, :]
    return dict(
        w_a=w_in_b[:, :, :OFF_G], w_g=w_in_b[:, :, OFF_G:],
        w_f=w_fourier.astype(BF16), w_s=w_sgu.astype(BF16), w_d=w_diff.astype(BF16),
        w_o=w_out.astype(BF16),
        vn_g=row(vn_g), vn_b=row(vn_b),
        sgu_wcat=jnp.transpose(sgu_w, (0, 2, 1, 3)).reshape(DEPTH, CHUNK, SGU_HEADS * CHUNK).astype(BF16),
        sgu_bias=jnp.repeat(jnp.transpose(sgu_b, (0, 2, 1)).astype(F32), SGU_HEAD_DIM, axis=2),
        lam=jnp.stack([lam_q1, lam_k1, lam_q2, lam_k2], axis=1).astype(F32),
        subln_g=row(subln_g), ln1_g=row(ln1_g), ln1_b=row(ln1_b), ln2_g=row(ln2_g), ln2_b=row(ln2_b),
        ffn_gate=_pad_ff(ffn_w_gate, 2), ffn_up=_pad_ff(ffn_w_up, 2), ffn_down=_pad_ff(ffn_w_down, 1),
        w_router=jnp.pad(w_router.astype(F32), ((0, 0), (0, 0), (0, ROUTER_PAD - N_EXPERTS))),
        moe_gate=_pad_ff(moe_w_gate, 3), moe_up=_pad_ff(moe_w_up, 3), moe_down=_pad_ff(moe_w_down, 2),
        dft64=_dft64_table(),
    )


def _trunk(x3, p):
    B, S, _ = x3.shape
    x = x3.reshape(B * S, D_MODEL)
    rope_cos, rope_sin = _rope_tables(S)
    ct, snt = _dft_tables(S)
    for l in range(DEPTH):
        lambda_init = 0.8 - 0.6 * math.exp(-0.3 * l)
        g, so, q, kt, va = _inproj(x, p["w_a"][l], p["dft64"], p["vn_g"][l], p["vn_b"][l],
                                   p["sgu_wcat"][l], p["sgu_bias"][l], rope_cos, rope_sin, S)
        fo = _fourier(g, ct, snt, B, S)
        do = _attn(p["lam"][l], q, kt, va, p["subln_g"][l], B, S, lambda_init)
        j = l // 2
        moe = l % 2 == 1
        x1, cw = _merge(x, fo, so, do, p["w_g"][l], p["w_f"][l], p["w_s"][l], p["w_d"][l], p["w_o"][l],
                        p["ln1_g"][l], p["ln1_b"][l], p["w_router"][j] if moe else None, S)
        if moe:
            x = _ffn(x1, cw, p["moe_gate"][j], p["moe_up"][j], p["moe_down"][j],
                     p["ln2_g"][l], p["ln2_b"][l], S)
        else:
            x = _ffn(x1, None, p["ffn_gate"][j:j + 1], p["ffn_up"][j:j + 1], p["ffn_down"][j:j + 1],
                     p["ln2_g"][l], p["ln2_b"][l], S)
    return x.reshape(B, S, D_MODEL)


def kernel(x_prompt, x_sample, w_in, w_fourier, w_sgu, w_diff, w_out, vn_g, vn_b, sgu_w, sgu_b,
           lam_q1, lam_k1, lam_q2, lam_k2, subln_g, ln1_g, ln1_b, ln2_g, ln2_b,
           ffn_w_gate, ffn_w_up, ffn_w_down, w_router, moe_w_gate, moe_w_up, moe_w_down):
    p = _prepare(w_in, w_fourier, w_sgu, w_diff, w_out, vn_g, vn_b, sgu_w, sgu_b,
                 lam_q1, lam_k1, lam_q2, lam_k2, subln_g, ln1_g, ln1_b, ln2_g, ln2_b,
                 ffn_w_gate, ffn_w_up, ffn_w_down, w_router, moe_w_gate, moe_w_up, moe_w_down)
    return (_trunk(x_prompt, p), _trunk(x_sample, p))
```

```python
import functools
import math

import jax
import jax.numpy as jnp
from jax import lax
from jax.experimental import pallas as pl
from jax.experimental.pallas import tpu as pltpu

F32 = jnp.float32
BF16 = jnp.bfloat16

D_MODEL = 1024
DEPTH = 2
N_FOURIER_GROUPS = 4
FOURIER_GROUP = 64
FOURIER_WIDTH = N_FOURIER_GROUPS * FOURIER_GROUP
SGU_HEADS = 4
SGU_HEAD_DIM = 64
SGU_WIDTH = SGU_HEADS * SGU_HEAD_DIM
CHUNK = 128
DIFF_HEADS = 4
DIFF_HEAD_DIM = 64
DIFF_V_DIM = 2 * DIFF_HEAD_DIM
DIFF_QK_WIDTH = DIFF_HEADS * 2 * DIFF_HEAD_DIM
DIFF_V_WIDTH = DIFF_HEADS * DIFF_V_DIM
ROPE_THETA = 10000.0
N_BRANCHES = 3
D_FF = 2752
N_EXPERTS = 8
ALPHA = (2 * DEPTH) ** 0.25
LN_EPS = 1e-5
RMS_EPS = 1e-5

OFF_F = 0
OFF_U = OFF_F + FOURIER_WIDTH
OFF_V = OFF_U + SGU_WIDTH
OFF_Q = OFF_V + SGU_WIDTH
OFF_K = OFF_Q + DIFF_QK_WIDTH
OFF_VA = OFF_K + DIFF_QK_WIDTH
OFF_G = OFF_VA + DIFF_V_WIDTH
IN_WIDTH = OFF_G + N_BRANCHES * D_MODEL

TOP_K = 2
LANES = 128
SUBLANES = 8
MOE_ROWS = 512
D_FF_PAD = 2816
ROUTER_PAD = LANES
MIB = 1024 * 1024


def _tiles(S):
    t = dict(
        tm_in=min(512, S),
        tm_comb=min(256, S),
        tm_dft=min(256, S),
        tq=min(max((2 * MIB) // S, 128), S),
        tk=min(1024, S),
        tf=D_FF_PAD // 2,
    )
    return t


def _layer_norm(z, g, b):
    mu = jnp.mean(z, axis=-1, keepdims=True)
    zc = z - mu
    var = jnp.mean(zc * zc, axis=-1, keepdims=True)
    return zc * lax.rsqrt(var + LN_EPS) * g + b


def _dot(a, b):
    return jnp.dot(a, b, preferred_element_type=F32)


def _inproj_kernel(x_ref, w_ref, dft_ref, vng_ref, vnb_ref, sguw_ref, sgub_ref, cos_ref, sin_ref,
                   g_out, so_out, q_out, kt_out, va_out):
    tm = x_ref.shape[0]
    xb = x_ref[...].astype(BF16)

    def proj(a, b):
        return _dot(xb, w_ref[:, a:b])

    f = proj(OFF_F, OFF_U)
    g_out[...] = _dot(f.astype(BF16), dft_ref[...]).astype(BF16)

    u = proj(OFF_U, OFF_V)
    v = proj(OFF_V, OFF_Q)
    vn = _layer_norm(v, vng_ref[...], vnb_ref[...]).astype(BF16)
    row_head = lax.broadcasted_iota(jnp.int32, (SGU_HEADS * CHUNK, SGU_WIDTH), 0) // CHUNK
    col_head = lax.broadcasted_iota(jnp.int32, (SGU_HEADS * CHUNK, SGU_WIDTH), 1) // SGU_HEAD_DIM
    head_mask = row_head == col_head
    for c in range(tm // CHUNK):
        rows = slice(c * CHUNK, (c + 1) * CHUNK)
        vb = vn[rows, :]
        rhs = jnp.where(head_mask, jnp.concatenate([vb] * SGU_HEADS, axis=0), jnp.zeros((), BF16))
        mixed = _dot(sguw_ref[...], rhs) + sgub_ref[...]
        so_out[rows, :] = (u[rows, :] * mixed).astype(BF16)

    cs = cos_ref[...]
    sn = sin_ref[...]
    lane = lax.broadcasted_iota(jnp.int32, (tm, LANES), 1)
    first_half = (lane % DIFF_HEAD_DIM) < (DIFF_HEAD_DIM // 2)

    def rope(t):
        outs = []
        for j in range(DIFF_QK_WIDTH // LANES):
            tc = t[:, j * LANES:(j + 1) * LANES]
            rot = jnp.where(first_half,
                            pltpu.roll(tc, LANES - DIFF_HEAD_DIM // 2, 1),
                            pltpu.roll(tc, DIFF_HEAD_DIM // 2, 1))
            outs.append(tc * cs + rot * sn)
        return jnp.concatenate(outs, axis=1)

    q = rope(proj(OFF_Q, OFF_K)) * (DIFF_HEAD_DIM ** -0.5)
    q_out[...] = q.astype(BF16)
    k = rope(proj(OFF_K, OFF_VA))
    kt_out[...] = k.T.astype(BF16)
    va_out[...] = proj(OFF_VA, OFF_G).astype(BF16)


def _inproj(x, w_a, dft64, vn_g, vn_b, sgu_wcat, sgu_bias, rope_cos, rope_sin, S):
    T = x.shape[0]
    tm = _tiles(S)["tm_in"]
    n_pos = S // tm
    full = lambda shape: pl.BlockSpec(shape, lambda i: (0,) * len(shape))
    return pl.pallas_call(
        _inproj_kernel,
        grid=(T // tm,),
        in_specs=[
            pl.BlockSpec((tm, D_MODEL), lambda i: (i, 0)),
            full((D_MODEL, OFF_G)),
            full((FOURIER_WIDTH, 2 * FOURIER_WIDTH)),
            full((1, SGU_WIDTH)),
            full((1, SGU_WIDTH)),
            full((CHUNK, SGU_HEADS * CHUNK)),
            full((CHUNK, SGU_WIDTH)),
            pl.BlockSpec((tm, LANES), lambda i: (i % n_pos, 0)),
            pl.BlockSpec((tm, LANES), lambda i: (i % n_pos, 0)),
        ],
        out_specs=[
            pl.BlockSpec((tm, 2 * FOURIER_WIDTH), lambda i: (i, 0)),
            pl.BlockSpec((tm, SGU_WIDTH), lambda i: (i, 0)),
            pl.BlockSpec((tm, DIFF_QK_WIDTH), lambda i: (i, 0)),
            pl.BlockSpec((DIFF_QK_WIDTH, tm), lambda i: (0, i)),
            pl.BlockSpec((tm, DIFF_V_WIDTH), lambda i: (i, 0)),
        ],
        out_shape=[
            jax.ShapeDtypeStruct((T, 2 * FOURIER_WIDTH), BF16),
            jax.ShapeDtypeStruct((T, SGU_WIDTH), BF16),
            jax.ShapeDtypeStruct((T, DIFF_QK_WIDTH), BF16),
            jax.ShapeDtypeStruct((DIFF_QK_WIDTH, T), BF16),
            jax.ShapeDtypeStruct((T, DIFF_V_WIDTH), BF16),
        ],
        compiler_params=pltpu.CompilerParams(
            dimension_semantics=("parallel",), vmem_limit_bytes=48 * MIB),
        name="inproj",
    )(x, w_a, dft64, vn_g, vn_b, sgu_wcat, sgu_bias, rope_cos, rope_sin)


def _fourier_kernel(ct_ref, snt_ref, g_ref, o_ref):
    acc = _dot(ct_ref[...], g_ref[:, :FOURIER_WIDTH])
    acc = acc + _dot(snt_ref[...], g_ref[:, FOURIER_WIDTH:])
    o_ref[...] = acc.astype(BF16)


def _fourier(g, ct, snt, B, S):
    tm = _tiles(S)["tm_dft"]
    n_row = S // tm
    return pl.pallas_call(
        _fourier_kernel,
        grid=(n_row, B),
        in_specs=[
            pl.BlockSpec((tm, S), lambda i, b: (i, 0)),
            pl.BlockSpec((tm, S), lambda i, b: (i, 0)),
            pl.BlockSpec((S, 2 * FOURIER_WIDTH), lambda i, b: (b, 0)),
        ],
        out_specs=pl.BlockSpec((tm, FOURIER_WIDTH), lambda i, b: (b * n_row + i, 0)),
        out_shape=jax.ShapeDtypeStruct((B * S, FOURIER_WIDTH), BF16),
        compiler_params=pltpu.CompilerParams(
            dimension_semantics=("parallel", "parallel"), vmem_limit_bytes=48 * MIB),
        name="fourier",
    )(ct, snt, g)


def _attn_kernel(lam_ref, q_ref, kt_ref, v_ref, g_ref, o_ref, e1_ref, e2_ref, *, tk, lambda_init):
    tq = q_ref.shape[0]
    S = kt_ref.shape[1]
    n_chunks = S // tk
    lp = lam_ref[...]
    lam = (jnp.exp(jnp.sum(lp[0:1, :] * lp[1:2, :], axis=-1, keepdims=True))
           - jnp.exp(jnp.sum(lp[2:3, :] * lp[3:4, :], axis=-1, keepdims=True))
           + lambda_init)
    q = q_ref[...]

    def scores_pass(qm, row0, e_ref):
        m = jnp.full((tq, 1), -jnp.inf, F32)
        l = jnp.zeros((tq, 1), F32)
        used = []
        for c in range(n_chunks):
            cols = slice(c * tk, (c + 1) * tk)
            s = _dot(qm, kt_ref[row0:row0 + DIFF_HEAD_DIM, cols])
            m_new = jnp.maximum(m, jnp.max(s, axis=-1, keepdims=True))
            e = jnp.exp(s - m_new)
            l = l * jnp.exp(m - m_new) + jnp.sum(e, axis=-1, keepdims=True)
            e_ref[:, cols] = e
            used.append(m_new)
            m = m_new
        return m, l, used

    m1, l1, used1 = scores_pass(q[:, :DIFF_HEAD_DIM], 0, e1_ref)
    m2, l2, used2 = scores_pass(q[:, DIFF_HEAD_DIM:], DIFF_HEAD_DIM, e2_ref)

    acc = jnp.zeros((tq, DIFF_V_DIM), F32)
    for c in range(n_chunks):
        cols = slice(c * tk, (c + 1) * tk)
        f1 = jnp.exp(used1[c] - m1) / l1
        f2 = lam * jnp.exp(used2[c] - m2) / l2
        a = e1_ref[:, cols] * f1 - e2_ref[:, cols] * f2
        acc = acc + _dot(a.astype(BF16), v_ref[cols, :])

    o = acc * lax.rsqrt(jnp.mean(acc * acc, axis=-1, keepdims=True) + RMS_EPS) * g_ref[...]
    o_ref[...] = (o * (1.0 - lambda_init)).astype(BF16)


def _attn(lam_params, q, kt, va, subln_g, B, S, lambda_init):
    t = _tiles(S)
    tq, tk = t["tq"], t["tk"]
    n_q = S // tq
    kern = functools.partial(_attn_kernel, tk=tk, lambda_init=lambda_init)
    return pl.pallas_call(
        kern,
        grid=(B, DIFF_HEADS, n_q),
        in_specs=[
            pl.BlockSpec((4, DIFF_HEAD_DIM), lambda b, h, i: (0, 0)),
            pl.BlockSpec((tq, DIFF_V_DIM), lambda b, h, i: (b * n_q + i, h)),
            pl.BlockSpec((DIFF_V_DIM, S), lambda b, h, i: (h, b)),
            pl.BlockSpec((S, DIFF_V_DIM), lambda b, h, i: (b, h)),
            pl.BlockSpec((1, DIFF_V_DIM), lambda b, h, i: (0, 0)),
        ],
        out_specs=pl.BlockSpec((tq, DIFF_V_DIM), lambda b, h, i: (b * n_q + i, h)),
        out_shape=jax.ShapeDtypeStruct((B * S, DIFF_V_WIDTH), BF16),
        scratch_shapes=[pltpu.VMEM((tq, S), F32), pltpu.VMEM((tq, S), F32)],
        compiler_params=pltpu.CompilerParams(
            dimension_semantics=("parallel", "parallel", "parallel"), vmem_limit_bytes=48 * MIB),
        name="attn",
    )(lam_params, q, kt, va, subln_g)


def _merge_kernel(*refs, with_router):
    if with_router:
        (x_ref, fo_ref, so_ref, do_ref, wg_ref, wf_ref, ws_ref, wd_ref, wo_ref, lg_ref, lb_ref,
         wr_ref, x1_out, route_out, counts_out, base_ref) = refs
    else:
        (x_ref, fo_ref, so_ref, do_ref, wg_ref, wf_ref, ws_ref, wd_ref, wo_ref, lg_ref, lb_ref,
         x1_out) = refs
    x = x_ref[...]
    xb = x.astype(BF16)
    merged = None
    for j, (br_ref, wb_ref) in enumerate(((fo_ref, wf_ref), (so_ref, ws_ref), (do_ref, wd_ref))):
        gate = jax.nn.sigmoid(_dot(xb, wg_ref[:, j * D_MODEL:(j + 1) * D_MODEL]))
        term = gate * _dot(br_ref[...], wb_ref[...])
        merged = term if merged is None else merged + term
    y = _dot(merged.astype(BF16), wo_ref[...])
    x1 = _layer_norm(ALPHA * x + y, lg_ref[...], lb_ref[...])
    x1_out[...] = x1

    if with_router:
        tm = x.shape[0]
        xh = x1.astype(BF16)
        xl = (x1 - xh.astype(F32)).astype(BF16)
        logits = _dot(jnp.concatenate([xh, xl, xh], axis=1), wr_ref[...])
        lane = lax.broadcasted_iota(jnp.int32, (tm, ROUTER_PAD), 1)
        neg = jnp.float32(-jnp.inf)
        lg = jnp.where(lane < N_EXPERTS, logits, neg)
        v1 = jnp.max(lg, axis=-1, keepdims=True)
        i1 = jnp.min(jnp.where(lg == v1, lane, ROUTER_PAD), axis=-1, keepdims=True)
        lg2 = jnp.where(lane == i1, neg, lg)
        v2 = jnp.max(lg2, axis=-1, keepdims=True)
        i2 = jnp.min(jnp.where(lg2 == v2, lane, ROUTER_PAD), axis=-1, keepdims=True)
        e2 = jnp.exp(v2 - v1)
        den = 1.0 + e2

        @pl.when(pl.program_id(0) == 0)
        def _():
            base_ref[...] = jnp.zeros_like(base_ref)

        pick1 = lane == i1
        pick2 = lane == i2
        sel = jnp.logical_or(pick1, pick2)
        r_id = lax.broadcasted_iota(jnp.int32, (tm, tm), 0)
        c_id = lax.broadcasted_iota(jnp.int32, (tm, tm), 1)
        ltri = (c_id < r_id).astype(BF16)
        base = base_ref[0:1, :]
        rank = base + _dot(ltri, sel.astype(BF16))
        r1 = jnp.sum(jnp.where(pick1, rank, 0.0), axis=-1, keepdims=True)
        r2 = jnp.sum(jnp.where(pick2, rank, 0.0), axis=-1, keepdims=True)
        new_base = base + jnp.sum(sel.astype(F32), axis=0, keepdims=True)
        base_ref[...] = jnp.broadcast_to(new_base, base_ref.shape)
        counts_out[...] = jnp.broadcast_to(new_base, counts_out.shape)
        fields = (1.0 / den, e2 / den, i1.astype(F32), i2.astype(F32), r1, r2)
        route = jnp.zeros((tm, ROUTER_PAD), F32)
        for n, val in enumerate(fields):
            route = jnp.where(lane == n, val, route)
        route_out[...] = route


def _merge(x, fo, so, do, w_g, w_f, w_s, w_d, w_o, ln_g, ln_b, w_router, S):
    T = x.shape[0]
    tm = _tiles(S)["tm_in"]
    with_router = w_router is not None
    full = lambda shape: pl.BlockSpec(shape, lambda i: (0,) * len(shape))
    row = lambda width: pl.BlockSpec((tm, width), lambda i: (i, 0))
    in_specs = [
        row(D_MODEL), row(FOURIER_WIDTH), row(SGU_WIDTH), row(DIFF_V_WIDTH),
        full((D_MODEL, N_BRANCHES * D_MODEL)),
        full((FOURIER_WIDTH, D_MODEL)), full((SGU_WIDTH, D_MODEL)), full((DIFF_V_WIDTH, D_MODEL)),
        full((D_MODEL, D_MODEL)), full((1, D_MODEL)), full((1, D_MODEL)),
    ]
    args = [x, fo, so, do, w_g, w_f, w_s, w_d, w_o, ln_g, ln_b]
    out_specs = [row(D_MODEL)]
    out_shape = [jax.ShapeDtypeStruct((T, D_MODEL), F32)]
    scratch = []
    if with_router:
        in_specs.append(full((3 * D_MODEL, ROUTER_PAD)))
        args.append(w_router)
        out_specs += [row(ROUTER_PAD), full((SUBLANES, ROUTER_PAD))]
        out_shape += [jax.ShapeDtypeStruct((T, ROUTER_PAD), F32),
                      jax.ShapeDtypeStruct((SUBLANES, ROUTER_PAD), F32)]
        scratch.append(pltpu.VMEM((SUBLANES, ROUTER_PAD), F32))
    outs = pl.pallas_call(
        functools.partial(_merge_kernel, with_router=with_router),
        grid=(T // tm,),
        in_specs=in_specs,
        out_specs=out_specs,
        out_shape=out_shape,
        scratch_shapes=scratch,
        compiler_params=pltpu.CompilerParams(
            dimension_semantics=("arbitrary" if with_router else "parallel",),
            vmem_limit_bytes=56 * MIB),
        name="merge_router" if with_router else "merge",
    )(*args)
    return (outs[0], outs[1], outs[2]) if with_router else (outs[0], None, None)


def _ffn_kernel(x_ref, wg_ref, wu_ref, wd_ref, lg_ref, lb_ref, o_ref, xb_ref, acc_ref):
    f = pl.program_id(1)

    @pl.when(f == 0)
    def _():
        xb_ref[...] = x_ref[...].astype(BF16)
        acc_ref[...] = jnp.zeros_like(acc_ref)

    xb = xb_ref[...]
    h = jax.nn.silu(_dot(xb, wg_ref[...])) * _dot(xb, wu_ref[...])
    acc_ref[...] += _dot(h.astype(BF16), wd_ref[...])

    @pl.when(f == pl.num_programs(1) - 1)
    def _():
        o_ref[...] = _layer_norm(ALPHA * x_ref[...] + acc_ref[...], lg_ref[...], lb_ref[...])


def _ffn(x, w_gate, w_up, w_down, ln_g, ln_b, S):
    T = x.shape[0]
    t = _tiles(S)
    tm, tf = t["tm_in"], t["tf"]
    return pl.pallas_call(
        _ffn_kernel,
        grid=(T // tm, D_FF_PAD // tf),
        in_specs=[
            pl.BlockSpec((tm, D_MODEL), lambda i, f: (i, 0)),
            pl.BlockSpec((D_MODEL, tf), lambda i, f: (0, f)),
            pl.BlockSpec((D_MODEL, tf), lambda i, f: (0, f)),
            pl.BlockSpec((tf, D_MODEL), lambda i, f: (f, 0)),
            pl.BlockSpec((1, D_MODEL), lambda i, f: (0, 0)),
            pl.BlockSpec((1, D_MODEL), lambda i, f: (0, 0)),
        ],
        out_specs=pl.BlockSpec((tm, D_MODEL), lambda i, f: (i, 0)),
        out_shape=jax.ShapeDtypeStruct((T, D_MODEL), F32),
        scratch_shapes=[pltpu.VMEM((tm, D_MODEL), BF16), pltpu.VMEM((tm, D_MODEL), F32)],
        compiler_params=pltpu.CompilerParams(
            dimension_semantics=("parallel", "arbitrary"), vmem_limit_bytes=56 * MIB),
        name="ffn_dense",
    )(x, w_gate, w_up, w_down, ln_g, ln_b)


def _route_tables(route, counts, T):
    R = MOE_ROWS
    as_int = lambda n: route[:, n].astype(jnp.int32)
    i1, i2, r1, r2 = as_int(2), as_int(3), as_int(4), as_int(5)
    cnt = counts[0, :N_EXPERTS].astype(jnp.int32)
    padded = (cnt + R - 1) // R * R
    ends = jnp.cumsum(padded)
    off = ends - padded
    n_tiles = TOP_K * T // R + N_EXPERTS
    tile_expert = jnp.minimum(
        jnp.searchsorted(ends, jnp.arange(n_tiles, dtype=jnp.int32) * R, side="right"),
        N_EXPERTS - 1).astype(jnp.int32)
    pad_start = off + cnt
    pad_n = (padded - cnt).at[N_EXPERTS - 1].set(n_tiles * R - pad_start[N_EXPERTS - 1])
    return dict(
        pos1=jnp.take(off, i1) + r1, pos2=jnp.take(off, i2) + r2,
        tile_expert=tile_expert, n_used=(ends[-1:] // R).astype(jnp.int32),
        pad=jnp.concatenate([pad_start, pad_n]).astype(jnp.int32), n_tiles=n_tiles)


def _dispatch_kernel(pad_ref, x_ref, p1_ref, p2_ref, xs_ref, zero_ref, sem):
    tm = x_ref.shape[0]

    def row_copy(src_ref, src_row, dst_row):
        return pltpu.make_async_copy(src_ref.at[pl.ds(src_row, 1)], xs_ref.at[pl.ds(dst_row, 1)], sem)

    @pl.when(pl.program_id(0) == 0)
    def _():
        zero_ref[...] = jnp.zeros_like(zero_ref)
        for e in range(N_EXPERTS):
            start = pad_ref[e]
            n = pad_ref[N_EXPERTS + e]

            def issue_pad(k, c):
                row_copy(zero_ref, 0, start + k).start()
                return c

            def drain_pad(k, c):
                row_copy(zero_ref, 0, start + k).wait()
                return c

            lax.fori_loop(0, n, issue_pad, 0)
            lax.fori_loop(0, n, drain_pad, 0)

    def issue(t, c):
        row_copy(x_ref, t, p1_ref[0, 0, t]).start()
        row_copy(x_ref, t, p2_ref[0, 0, t]).start()
        return c

    def drain(t, c):
        row_copy(x_ref, t, p1_ref[0, 0, t]).wait()
        row_copy(x_ref, t, p2_ref[0, 0, t]).wait()
        return c

    lax.fori_loop(0, tm, issue, 0, unroll=8)
    lax.fori_loop(0, tm, drain, 0, unroll=8)


def _dispatch(x, tabs, S):
    T = x.shape[0]
    tm = _tiles(S)["tm_in"]
    n_rows = tabs["n_tiles"] * MOE_ROWS
    idx_spec = pl.BlockSpec((1, 1, tm), lambda i, pad: (i, 0, 0), memory_space=pltpu.SMEM)
    return pl.pallas_call(
        _dispatch_kernel,
        grid_spec=pltpu.PrefetchScalarGridSpec(
            num_scalar_prefetch=1,
            grid=(T // tm,),
            in_specs=[pl.BlockSpec((tm, D_MODEL), lambda i, pad: (i, 0)), idx_spec, idx_spec],
            out_specs=pl.BlockSpec(memory_space=pl.ANY),
            scratch_shapes=[pltpu.VMEM((SUBLANES, D_MODEL), F32), pltpu.SemaphoreType.DMA(())],
        ),
        out_shape=jax.ShapeDtypeStruct((n_rows, D_MODEL), F32),
        compiler_params=pltpu.CompilerParams(dimension_semantics=("arbitrary",)),
        name="moe_dispatch",
    )(tabs["pad"], x, tabs["pos1"].reshape(T // tm, 1, tm), tabs["pos2"].reshape(T // tm, 1, tm))


def _gmm_kernel(te_ref, nu_ref, x_ref, wg_ref, wu_ref, wd_ref, o_ref, *, tf):
    used = pl.program_id(0) < nu_ref[0]

    @pl.when(jnp.logical_not(used))
    def _():
        o_ref[...] = jnp.zeros_like(o_ref)

    @pl.when(used)
    def _():
        xb = x_ref[...].astype(BF16)
        y = None
        for f in range(D_FF_PAD // tf):
            cols = slice(f * tf, (f + 1) * tf)
            h = jax.nn.silu(_dot(xb, wg_ref[0, :, cols])) * _dot(xb, wu_ref[0, :, cols])
            part = _dot(h.astype(BF16), wd_ref[0, cols, :])
            y = part if y is None else y + part
        o_ref[...] = y


def _gmm(xs, tabs, w_gate, w_up, w_down, S):
    R = MOE_ROWS
    tf = _tiles(S)["tf"]
    row_spec = pl.BlockSpec((R, D_MODEL), lambda j, te, nu: (j, 0))
    w_spec = lambda shape: pl.BlockSpec(shape, lambda j, te, nu: (te[j], 0, 0),
                                        pipeline_mode=pl.Buffered(1))
    return pl.pallas_call(
        functools.partial(_gmm_kernel, tf=tf),
        grid_spec=pltpu.PrefetchScalarGridSpec(
            num_scalar_prefetch=2,
            grid=(tabs["n_tiles"],),
            in_specs=[row_spec, w_spec((1, D_MODEL, D_FF_PAD)), w_spec((1, D_MODEL, D_FF_PAD)),
                      w_spec((1, D_FF_PAD, D_MODEL))],
            out_specs=row_spec,
        ),
        out_shape=jax.ShapeDtypeStruct(xs.shape, F32),
        compiler_params=pltpu.CompilerParams(
            dimension_semantics=("arbitrary",), vmem_limit_bytes=56 * MIB),
        name="moe_gmm",
    )(tabs["tile_expert"], tabs["n_used"], xs, w_gate, w_up, w_down)


def _combine_kernel(x_ref, route_ref, p1_ref, p2_ref, lg_ref, lb_ref, ys_ref, o_ref, y1_ref, y2_ref, sem):
    tm = x_ref.shape[0]

    def row_copies(t):
        return (pltpu.make_async_copy(ys_ref.at[pl.ds(p1_ref[0, 0, t], 1)], y1_ref.at[pl.ds(t, 1)], sem),
                pltpu.make_async_copy(ys_ref.at[pl.ds(p2_ref[0, 0, t], 1)], y2_ref.at[pl.ds(t, 1)], sem))

    def issue(t, c):
        for cp in row_copies(t):
            cp.start()
        return c

    def drain(t, c):
        for cp in row_copies(t):
            cp.wait()
        return c

    lax.fori_loop(0, tm, issue, 0, unroll=8)
    lax.fori_loop(0, tm, drain, 0, unroll=8)
    route = route_ref[...]
    ff = route[:, 0:1] * y1_ref[...] + route[:, 1:2] * y2_ref[...]
    o_ref[...] = _layer_norm(ALPHA * x_ref[...] + ff, lg_ref[...], lb_ref[...])


def _combine(x, route, tabs, ys, ln_g, ln_b, S):
    T = x.shape[0]
    tm = _tiles(S)["tm_comb"]
    idx_spec = pl.BlockSpec((1, 1, tm), lambda i: (i, 0, 0), memory_space=pltpu.SMEM)
    return pl.pallas_call(
        _combine_kernel,
        grid=(T // tm,),
        in_specs=[
            pl.BlockSpec((tm, D_MODEL), lambda i: (i, 0)),
            pl.BlockSpec((tm, ROUTER_PAD), lambda i: (i, 0)),
            idx_spec, idx_spec,
            pl.BlockSpec((1, D_MODEL), lambda i: (0, 0)),
            pl.BlockSpec((1, D_MODEL), lambda i: (0, 0)),
            pl.BlockSpec(memory_space=pl.ANY),
        ],
        out_specs=pl.BlockSpec((tm, D_MODEL), lambda i: (i, 0)),
        out_shape=jax.ShapeDtypeStruct((T, D_MODEL), F32),
        scratch_shapes=[pltpu.VMEM((tm, D_MODEL), F32), pltpu.VMEM((tm, D_MODEL), F32),
                        pltpu.SemaphoreType.DMA(())],
        compiler_params=pltpu.CompilerParams(dimension_semantics=("arbitrary",)),
        name="moe_combine",
    )(x, route, tabs["pos1"].reshape(T // tm, 1, tm), tabs["pos2"].reshape(T // tm, 1, tm), ln_g, ln_b, ys)


def _rope_tables(S):
    half = DIFF_HEAD_DIM // 2
    inv = ROPE_THETA ** (-jnp.arange(0, DIFF_HEAD_DIM, 2, dtype=F32) / DIFF_HEAD_DIM)
    ang = jnp.arange(S, dtype=F32)[:, None] * inv[None, :]
    cos, sin = jnp.cos(ang), jnp.sin(ang)
    reps = LANES // DIFF_HEAD_DIM
    cos_t = jnp.tile(jnp.concatenate([cos, cos], axis=1), (1, reps))
    sin_t = jnp.tile(jnp.concatenate([-sin, sin], axis=1), (1, reps))
    assert cos_t.shape == (S, LANES) and half * 2 == DIFF_HEAD_DIM
    return cos_t, sin_t


def _dft_tables(S):
    j = jnp.arange(S, dtype=jnp.int32)
    ang = ((j[:, None] * j[None, :]) % S).astype(F32) * (2.0 * math.pi / S)
    scale = S ** -0.5
    return (jnp.cos(ang) * scale).astype(BF16), (jnp.sin(ang) * -scale).astype(BF16)


def _dft64_table():
    j = jnp.arange(FOURIER_GROUP, dtype=jnp.int32)
    ang = ((j[:, None] * j[None, :]) % FOURIER_GROUP).astype(F32) * (2.0 * math.pi / FOURIER_GROUP)
    scale = FOURIER_GROUP ** -0.5
    eye = jnp.eye(N_FOURIER_GROUPS, dtype=F32)
    return jnp.concatenate([jnp.kron(eye, jnp.cos(ang) * scale),
                            jnp.kron(eye, jnp.sin(ang) * scale)], axis=1).astype(BF16)


def _pad_ff(w, axis):
    pad = [(0, 0)] * w.ndim
    pad[axis] = (0, D_FF_PAD - D_FF)
    return jnp.pad(w, pad).astype(BF16)


def _split_router(w_router):
    w = jnp.pad(w_router.astype(F32), ((0, 0), (0, 0), (0, ROUTER_PAD - N_EXPERTS)))
    hi = w.astype(BF16)
    lo = (w - hi.astype(F32)).astype(BF16)
    return jnp.concatenate([hi, hi, lo], axis=1)


def _prepare(w_in,w_fourier, w_sgu, w_diff, w_out, vn_g, vn_b, sgu_w, sgu_b,
             lam_q1, lam_k1, lam_q2, lam_k2, subln_g, ln1_g, ln1_b, ln2_g, ln2_b,
             ffn_w_gate, ffn_w_up, ffn_w_down, w_router, moe_w_gate, moe_w_up, moe_w_down):
    w_in_b = w_in.astype(BF16)
    row = lambda a: a.astype(F32)[:, None, :]
    return dict(
        w_a=w_in_b[:, :, :OFF_G], w_g=w_in_b[:, :, OFF_G:],
        w_f=w_fourier.astype(BF16), w_s=w_sgu.astype(BF16), w_d=w_diff.astype(BF16),
        w_o=w_out.astype(BF16),
        vn_g=row(vn_g), vn_b=row(vn_b),
        sgu_wcat=jnp.transpose(sgu_w, (0, 2, 1, 3)).reshape(DEPTH, CHUNK, SGU_HEADS * CHUNK).astype(BF16),
        sgu_bias=jnp.repeat(jnp.transpose(sgu_b, (0, 2, 1)).astype(F32), SGU_HEAD_DIM, axis=2),
        lam=jnp.stack([lam_q1, lam_k1, lam_q2, lam_k2], axis=1).astype(F32),
        subln_g=row(subln_g), ln1_g=row(ln1_g), ln1_b=row(ln1_b), ln2_g=row(ln2_g), ln2_b=row(ln2_b),
        ffn_gate=_pad_ff(ffn_w_gate, 2), ffn_up=_pad_ff(ffn_w_up, 2), ffn_down=_pad_ff(ffn_w_down, 1),
        w_router=_split_router(w_router),
        moe_gate=_pad_ff(moe_w_gate, 3), moe_up=_pad_ff(moe_w_up, 3), moe_down=_pad_ff(moe_w_down, 2),
        dft64=_dft64_table(),
    )


def _trunk(x3, p):
    B, S, _ = x3.shape
    x = x3.reshape(B * S, D_MODEL)
    rope_cos, rope_sin = _rope_tables(S)
    ct, snt = _dft_tables(S)
    for l in range(DEPTH):
        lambda_init = 0.8 - 0.6 * math.exp(-0.3 * l)
        g, so, q, kt, va = _inproj(x, p["w_a"][l], p["dft64"], p["vn_g"][l], p["vn_b"][l],
                                   p["sgu_wcat"][l], p["sgu_bias"][l], rope_cos, rope_sin, S)
        fo = _fourier(g, ct, snt, B, S)
        do = _attn(p["lam"][l], q, kt, va, p["subln_g"][l], B, S, lambda_init)
        j = l // 2
        moe = l % 2 == 1
        x1, route, counts = _merge(x, fo, so, do, p["w_g"][l], p["w_f"][l], p["w_s"][l], p["w_d"][l],
                                   p["w_o"][l], p["ln1_g"][l], p["ln1_b"][l],
                                   p["w_router"][j] if moe else None, S)
        if moe:
            tabs = _route_tables(route, counts, B * S)
            xs = _dispatch(x1, tabs, S)
            ys = _gmm(xs, tabs, p["moe_gate"][j], p["moe_up"][j], p["moe_down"][j], S)
            x = _combine(x1, route, tabs, ys, p["ln2_g"][l], p["ln2_b"][l], S)
        else:
            x = _ffn(x1, p["ffn_gate"][j], p["ffn_up"][j], p["ffn_down"][j],
                     p["ln2_g"][l], p["ln2_b"][l], S)
    return x.reshape(B, S, D_MODEL)


def kernel(x_prompt, x_sample, w_in, w_fourier, w_sgu, w_diff, w_out, vn_g, vn_b, sgu_w, sgu_b,
           lam_q1, lam_k1, lam_q2, lam_k2, subln_g, ln1_g, ln1_b, ln2_g, ln2_b,
           ffn_w_gate, ffn_w_up, ffn_w_down, w_router, moe_w_gate, moe_w_up, moe_w_down):
    p = _prepare(w_in, w_fourier, w_sgu, w_diff, w_out, vn_g, vn_b, sgu_w, sgu_b,
                 lam_q1, lam_k1, lam_q2, lam_k2, subln_g, ln1_g, ln1_b, ln2_g, ln2_b,
                 ffn_w_gate, ffn_w_up, ffn_w_down, w_router, moe_w_gate, moe_w_up, moe_w_down)
    return (_trunk(x_prompt, p), _trunk(x_sample, p))
```

```python
import functools
import math

import jax
import jax.numpy as jnp
from jax import lax
from jax.experimental import pallas as pl
from jax.experimental.pallas import tpu as pltpu

F32 = jnp.float32
BF16 = jnp.bfloat16

D_MODEL = 1024
DEPTH = 2
N_FOURIER_GROUPS = 4
FOURIER_GROUP = 64
FOURIER_WIDTH = N_FOURIER_GROUPS * FOURIER_GROUP
SGU_HEADS = 4
SGU_HEAD_DIM = 64
SGU_WIDTH = SGU_HEADS * SGU_HEAD_DIM
CHUNK = 128
DIFF_HEADS = 4
DIFF_HEAD_DIM = 64
DIFF_V_DIM = 2 * DIFF_HEAD_DIM
DIFF_QK_WIDTH = DIFF_HEADS * 2 * DIFF_HEAD_DIM
DIFF_V_WIDTH = DIFF_HEADS * DIFF_V_DIM
ROPE_THETA = 10000.0
N_BRANCHES = 3
D_FF = 2752
N_EXPERTS = 8
ALPHA = (2 * DEPTH) ** 0.25
LN_EPS = 1e-5
RMS_EPS = 1e-5

OFF_F = 0
OFF_U = OFF_F + FOURIER_WIDTH
OFF_V = OFF_U + SGU_WIDTH
OFF_Q = OFF_V + SGU_WIDTH
OFF_K = OFF_Q + DIFF_QK_WIDTH
OFF_VA = OFF_K + DIFF_QK_WIDTH
OFF_G = OFF_VA + DIFF_V_WIDTH
IN_WIDTH = OFF_G + N_BRANCHES * D_MODEL

TOP_K = 2
LANES = 128
SUBLANES = 8
MOE_ROWS = 512
DFT_ROW_SPLIT = 64
SCORE_LEAD = 4
D_FF_PAD = 2816
ROUTER_PAD = LANES
MIB = 1024 * 1024


def _tiles(S):
    t = dict(
        tm_in=min(512, S),
        tm_comb=min(256, S),
        tm_dft=min(256, S),
        tq=min(512, S),
        tkc=min(128, S),
        tkp=min(512, S),
        tf=D_FF_PAD // 2,
    )
    return t


def _layer_norm(z, g, b):
    mu = jnp.mean(z, axis=-1, keepdims=True)
    zc = z - mu
    var = jnp.mean(zc * zc, axis=-1, keepdims=True)
    return zc * lax.rsqrt(var + LN_EPS) * g + b


def _dot(a, b):
    return jnp.dot(a, b, preferred_element_type=F32)


def _inproj_kernel(x_ref, w_ref, dft_ref, vng_ref, vnb_ref, sguw_ref, sgub_ref, cos_ref, sin_ref,
                   g_out, so_out, qt_out, k_out, vat_out):
    tm = x_ref.shape[0]
    xb = x_ref[...].astype(BF16)

    def proj(a, b):
        return _dot(xb, w_ref[:, a:b])

    f = proj(OFF_F, OFF_U)
    g_out[...] = _dot(f.astype(BF16), dft_ref[...]).astype(BF16)

    u = proj(OFF_U, OFF_V)
    v = proj(OFF_V, OFF_Q)
    vn = _layer_norm(v, vng_ref[...], vnb_ref[...]).astype(BF16)
    row_head = lax.broadcasted_iota(jnp.int32, (SGU_HEADS * CHUNK, SGU_WIDTH), 0) // CHUNK
    col_head = lax.broadcasted_iota(jnp.int32, (SGU_HEADS * CHUNK, SGU_WIDTH), 1) // SGU_HEAD_DIM
    head_mask = row_head == col_head
    for c in range(tm // CHUNK):
        rows = slice(c * CHUNK, (c + 1) * CHUNK)
        vb = vn[rows, :]
        rhs = jnp.where(head_mask, jnp.concatenate([vb] * SGU_HEADS, axis=0), jnp.zeros((), BF16))
        mixed = _dot(sguw_ref[...], rhs) + sgub_ref[...]
        so_out[rows, :] = (u[rows, :] * mixed).astype(BF16)

    cs = cos_ref[...]
    sn = sin_ref[...]
    lane = lax.broadcasted_iota(jnp.int32, (tm, LANES), 1)
    first_half = (lane % DIFF_HEAD_DIM) < (DIFF_HEAD_DIM // 2)

    def rope(t):
        outs = []
        for j in range(DIFF_QK_WIDTH // LANES):
            tc = t[:, j * LANES:(j + 1) * LANES]
            rot = jnp.where(first_half,
                            pltpu.roll(tc, LANES - DIFF_HEAD_DIM // 2, 1),
                            pltpu.roll(tc, DIFF_HEAD_DIM // 2, 1))
            outs.append(tc * cs + rot * sn)
        return jnp.concatenate(outs, axis=1)

    q = rope(proj(OFF_Q, OFF_K)) * (DIFF_HEAD_DIM ** -0.5)
    qt_out[...] = q.T.astype(BF16)
    k_out[...] = rope(proj(OFF_K, OFF_VA)).astype(BF16)
    vat_out[...] = proj(OFF_VA, OFF_G).T.astype(BF16)


def _inproj(x, w_a, dft64, vn_g, vn_b, sgu_wcat, sgu_bias, rope_cos, rope_sin, S):
    T = x.shape[0]
    tm = _tiles(S)["tm_in"]
    n_pos = S // tm
    full = lambda shape: pl.BlockSpec(shape, lambda i: (0,) * len(shape))
    return pl.pallas_call(
        _inproj_kernel,
        grid=(T // tm,),
        in_specs=[
            pl.BlockSpec((tm, D_MODEL), lambda i: (i, 0)),
            full((D_MODEL, OFF_G)),
            full((FOURIER_WIDTH, 2 * FOURIER_WIDTH)),
            full((1, SGU_WIDTH)),
            full((1, SGU_WIDTH)),
            full((CHUNK, SGU_HEADS * CHUNK)),
            full((CHUNK, SGU_WIDTH)),
            pl.BlockSpec((tm, LANES), lambda i: (i % n_pos, 0)),
            pl.BlockSpec((tm, LANES), lambda i: (i % n_pos, 0)),
        ],
        out_specs=[
            pl.BlockSpec((tm, 2 * FOURIER_WIDTH), lambda i: (i, 0)),
            pl.BlockSpec((tm, SGU_WIDTH), lambda i: (i, 0)),
            pl.BlockSpec((DIFF_QK_WIDTH, tm), lambda i: (0, i)),
            pl.BlockSpec((tm, DIFF_QK_WIDTH), lambda i: (i, 0)),
            pl.BlockSpec((DIFF_V_WIDTH, tm), lambda i: (0, i)),
        ],
        out_shape=[
            jax.ShapeDtypeStruct((T, 2 * FOURIER_WIDTH), BF16),
            jax.ShapeDtypeStruct((T, SGU_WIDTH), BF16),
            jax.ShapeDtypeStruct((DIFF_QK_WIDTH, T), BF16),
            jax.ShapeDtypeStruct((T, DIFF_QK_WIDTH), BF16),
            jax.ShapeDtypeStruct((DIFF_V_WIDTH, T), BF16),
        ],
        compiler_params=pltpu.CompilerParams(
            dimension_semantics=("parallel",), vmem_limit_bytes=48 * MIB),
        name="inproj",
    )(x, w_a, dft64, vn_g, vn_b, sgu_wcat, sgu_bias, rope_cos, rope_sin)


def _fourier_kernel(ct_ref, snt_ref, g_ref, o_ref):
    acc = _dot(ct_ref[...], g_ref[:, :FOURIER_WIDTH])
    acc = acc + _dot(snt_ref[...], g_ref[:, FOURIER_WIDTH:])
    o_ref[...] = acc.astype(BF16)


def _fourier(g, ct, snt, B, S):
    tm = _tiles(S)["tm_dft"]
    n_row = S // tm
    return pl.pallas_call(
        _fourier_kernel,
        grid=(n_row, B),
        in_specs=[
            pl.BlockSpec((tm, S), lambda i, b: (i, 0)),
            pl.BlockSpec((tm, S), lambda i, b: (i, 0)),
            pl.BlockSpec((S, 2 * FOURIER_WIDTH), lambda i, b: (b, 0)),
        ],
        out_specs=pl.BlockSpec((tm, FOURIER_WIDTH), lambda i, b: (b * n_row + i, 0)),
        out_shape=jax.ShapeDtypeStruct((B * S, FOURIER_WIDTH), BF16),
        compiler_params=pltpu.CompilerParams(
            dimension_semantics=("parallel", "parallel"), vmem_limit_bytes=48 * MIB),
        name="fourier",
    )(ct, snt, g)


def _attn_kernel(lam_ref, qt_ref, k_ref, vt_ref, g_ref, o_ref, e1_ref, e2_ref, m1_ref, m2_ref,
                 *, tkc, tkp, lambda_init):
    tq = qt_ref.shape[1]
    S = k_ref.shape[0]
    groups = tkc // SUBLANES
    lp = lam_ref[...]
    lam = (jnp.exp(jnp.sum(lp[0:1, :] * lp[1:2, :], axis=-1, keepdims=True))
           - jnp.exp(jnp.sum(lp[2:3, :] * lp[3:4, :], axis=-1, keepdims=True))
           + lambda_init)
    qt = qt_ref[...]
    row = lax.broadcasted_iota(jnp.int32, qt.shape, 0)
    zero = jnp.zeros((), BF16)
    qt_both = jnp.concatenate(
        [jnp.where(row < DIFF_HEAD_DIM, qt, zero), jnp.where(row >= DIFF_HEAD_DIM, qt, zero)], axis=1)
    e_refs = (e1_ref, e2_ref)
    m_refs = (m1_ref, m2_ref)
    n_p = S // tkp
    sub_rows = lambda p: [slice(p * tkp + u * tkc, p * tkp + (u + 1) * tkc) for u in range(tkp // tkc)]
    tile3 = lambda ref, rows: ref[rows, :].reshape(groups, SUBLANES, tq)

    neg = jnp.full((SUBLANES, tq), -jnp.inf, F32)
    m_run = [neg, neg]
    m_exp = [neg, neg]
    l_run = [jnp.zeros((SUBLANES, tq), F32)] * 2
    l_hist = []
    for p in range(n_p + 1):
        if p < n_p:
            hold = 0
            if p >= SCORE_LEAD:
                bits = pltpu.bitcast(l_hist[p - SCORE_LEAD], jnp.int32)[0, 0]
                half = jnp.int32(16)
                hold = lax.shift_right_logical(lax.shift_right_logical(bits, half), half)
            kc = k_ref[pl.ds(pl.multiple_of(p * tkp + hold, tkp), tkp), :]
            s_both = _dot(kc, qt_both)
            for n in range(2):
                s = s_both[:, n * tq:(n + 1) * tq]
                e_refs[n][p * tkp:(p + 1) * tkp, :] = s
                m_run[n] = jnp.maximum(m_run[n], jnp.max(s.reshape(tkp // SUBLANES, SUBLANES, tq), axis=0))
            for n in range(2):
                m_refs[n][p] = m_run[n]
        if p >= 1:
            for n in range(2):
                m_new = m_refs[n][p - 1]
                part = None
                for rows in sub_rows(p - 1):
                    e = jnp.exp(tile3(e_refs[n], rows) - m_new[None])
                    e_refs[n][rows, :] = e.reshape(tkc, tq)
                    es = jnp.sum(e, axis=0)
                    part = es if part is None else part + es
                l_run[n] = l_run[n] * jnp.exp(m_exp[n] - m_new) + part
                m_exp[n] = m_new
            l_hist.append(l_run[1])

    def merge_classes(m, l):
        m_all = jnp.max(m, axis=0, keepdims=True)
        return m_all, jnp.sum(l * jnp.exp(m - m_all), axis=0, keepdims=True)

    mf1, lf1 = merge_classes(m_exp[0], l_run[0])
    mf2, lf2 = merge_classes(m_exp[1], l_run[1])
    inv1 = 1.0 / lf1
    inv2 = lam / lf2

    acc = jnp.zeros((DIFF_V_DIM, tq), F32)
    for p in range(n_p):
        f1 = jnp.exp(m1_ref[p] - mf1) * inv1
        f2 = jnp.exp(m2_ref[p] - mf2) * inv2
        parts = [(tile3(e1_ref, rows) * f1[None] - tile3(e2_ref, rows) * f2[None])
                 .reshape(tkc, tq).astype(BF16) for rows in sub_rows(p)]
        acc = acc + _dot(vt_ref[:, p * tkp:(p + 1) * tkp], jnp.concatenate(parts, axis=0))

    o = acc.T
    o = o * lax.rsqrt(jnp.mean(o * o, axis=-1, keepdims=True) + RMS_EPS) * g_ref[...]
    o_ref[...] = (o * (1.0 - lambda_init)).astype(BF16)


def _attn(lam_params, qt, k, vat, subln_g, B, S, lambda_init):
    t = _tiles(S)
    tq, tkc, tkp = t["tq"], t["tkc"], t["tkp"]
    n_q = S // tq
    kern = functools.partial(_attn_kernel, tkc=tkc, tkp=tkp, lambda_init=lambda_init)
    return pl.pallas_call(
        kern,
        grid=(B, DIFF_HEADS, n_q),
        in_specs=[
            pl.BlockSpec((4, DIFF_HEAD_DIM), lambda b, h, i: (0, 0)),
            pl.BlockSpec((DIFF_V_DIM, tq), lambda b, h, i: (h, b * n_q + i)),
            pl.BlockSpec((S, DIFF_V_DIM), lambda b, h, i: (b, h)),
            pl.BlockSpec((DIFF_V_DIM, S), lambda b, h, i: (h, b)),
            pl.BlockSpec((1, DIFF_V_DIM), lambda b, h, i: (0, 0)),
        ],
        out_specs=pl.BlockSpec((tq, DIFF_V_DIM), lambda b, h, i: (b * n_q + i, h)),
        out_shape=jax.ShapeDtypeStruct((B * S, DIFF_V_WIDTH), BF16),
        scratch_shapes=[pltpu.VMEM((S, tq), F32), pltpu.VMEM((S, tq), F32),
                        pltpu.VMEM((S // tkp, SUBLANES, tq), F32),
                        pltpu.VMEM((S // tkp, SUBLANES, tq), F32)],
        compiler_params=pltpu.CompilerParams(
            dimension_semantics=("parallel", "parallel", "parallel"), vmem_limit_bytes=48 * MIB),
        name="attn",
    )(lam_params, qt, k, vat, subln_g)


def _merge_kernel(*refs, with_router):
    if with_router:
        (x_ref, fo_ref, so_ref, do_ref, wg_ref, wf_ref, ws_ref, wd_ref, wo_ref, lg_ref, lb_ref,
         wr_ref, x1_out, route_out, counts_out, base_ref) = refs
    else:
        (x_ref, fo_ref, so_ref, do_ref, wg_ref, wf_ref, ws_ref, wd_ref, wo_ref, lg_ref, lb_ref,
         x1_out) = refs
    x = x_ref[...]
    xb = x.astype(BF16)
    merged = None
    for j, (br_ref, wb_ref) in enumerate(((fo_ref, wf_ref), (so_ref, ws_ref), (do_ref, wd_ref))):
        gate = jax.nn.sigmoid(_dot(xb, wg_ref[:, j * D_MODEL:(j + 1) * D_MODEL]))
        term = gate * _dot(br_ref[...], wb_ref[...])
        merged = term if merged is None else merged + term
    y = _dot(merged.astype(BF16), wo_ref[...])
    x1 = _layer_norm(ALPHA * x + y, lg_ref[...], lb_ref[...])
    x1_out[...] = x1

    if with_router:
        tm = x.shape[0]
        xh = x1.astype(BF16)
        xl = (x1 - xh.astype(F32)).astype(BF16)
        logits = _dot(jnp.concatenate([xh, xl, xh], axis=1), wr_ref[...])
        lane = lax.broadcasted_iota(jnp.int32, (tm, ROUTER_PAD), 1)
        neg = jnp.float32(-jnp.inf)
        lg = jnp.where(lane < N_EXPERTS, logits, neg)
        v1 = jnp.max(lg, axis=-1, keepdims=True)
        i1 = jnp.min(jnp.where(lg == v1, lane, ROUTER_PAD), axis=-1, keepdims=True)
        lg2 = jnp.where(lane == i1, neg, lg)
        v2 = jnp.max(lg2, axis=-1, keepdims=True)
        i2 = jnp.min(jnp.where(lg2 == v2, lane, ROUTER_PAD), axis=-1, keepdims=True)
        e2 = jnp.exp(v2 - v1)
        den = 1.0 + e2

        @pl.when(pl.program_id(0) == 0)
        def _():
            base_ref[...] = jnp.zeros_like(base_ref)

        pick1 = lane == i1
        pick2 = lane == i2
        sel = jnp.logical_or(pick1, pick2)
        r_id = lax.broadcasted_iota(jnp.int32, (tm, tm), 0)
        c_id = lax.broadcasted_iota(jnp.int32, (tm, tm), 1)
        ltri = (c_id < r_id).astype(BF16)
        base = base_ref[0:1, :]
        rank = base + _dot(ltri, sel.astype(BF16))
        r1 = jnp.sum(jnp.where(pick1, rank, 0.0), axis=-1, keepdims=True)
        r2 = jnp.sum(jnp.where(pick2, rank, 0.0), axis=-1, keepdims=True)
        new_base = base + jnp.sum(sel.astype(F32), axis=0, keepdims=True)
        base_ref[...] = jnp.broadcast_to(new_base, base_ref.shape)
        counts_out[...] = jnp.broadcast_to(new_base, counts_out.shape)
        fields = (1.0 / den, e2 / den, i1.astype(F32), i2.astype(F32), r1, r2)
        route = jnp.zeros((tm, ROUTER_PAD), F32)
        for n, val in enumerate(fields):
            route = jnp.where(lane == n, val, route)
        route_out[...] = route


def _merge(x, fo, so, do, w_g, w_f, w_s, w_d, w_o, ln_g, ln_b, w_router, S):
    T = x.shape[0]
    tm = _tiles(S)["tm_in"]
    with_router = w_router is not None
    full = lambda shape: pl.BlockSpec(shape, lambda i: (0,) * len(shape))
    row = lambda width: pl.BlockSpec((tm, width), lambda i: (i, 0))
    in_specs = [
        row(D_MODEL), row(FOURIER_WIDTH), row(SGU_WIDTH), row(DIFF_V_WIDTH),
        full((D_MODEL, N_BRANCHES * D_MODEL)),
        full((FOURIER_WIDTH, D_MODEL)), full((SGU_WIDTH, D_MODEL)), full((DIFF_V_WIDTH, D_MODEL)),
        full((D_MODEL, D_MODEL)), full((1, D_MODEL)), full((1, D_MODEL)),
    ]
    args = [x, fo, so, do, w_g, w_f, w_s, w_d, w_o, ln_g, ln_b]
    out_specs = [row(D_MODEL)]
    out_shape = [jax.ShapeDtypeStruct((T, D_MODEL), F32)]
    scratch = []
    if with_router:
        in_specs.append(full((3 * D_MODEL, ROUTER_PAD)))
        args.append(w_router)
        out_specs += [row(ROUTER_PAD), full((SUBLANES, ROUTER_PAD))]
        out_shape += [jax.ShapeDtypeStruct((T, ROUTER_PAD), F32),
                      jax.ShapeDtypeStruct((SUBLANES, ROUTER_PAD), F32)]
        scratch.append(pltpu.VMEM((SUBLANES, ROUTER_PAD), F32))
    outs = pl.pallas_call(
        functools.partial(_merge_kernel, with_router=with_router),
        grid=(T // tm,),
        in_specs=in_specs,
        out_specs=out_specs,
        out_shape=out_shape,
        scratch_shapes=scratch,
        compiler_params=pltpu.CompilerParams(
            dimension_semantics=("arbitrary" if with_router else "parallel",),
            vmem_limit_bytes=56 * MIB),
        name="merge_router" if with_router else "merge",
    )(*args)
    return (outs[0], outs[1], outs[2]) if with_router else (outs[0], None, None)


def _ffn_kernel(x_ref, wg_ref, wu_ref, wd_ref, lg_ref, lb_ref, o_ref, xb_ref, acc_ref):
    f = pl.program_id(1)

    @pl.when(f == 0)
    def _():
        xb_ref[...] = x_ref[...].astype(BF16)
        acc_ref[...] = jnp.zeros_like(acc_ref)

    xb = xb_ref[...]
    h = jax.nn.silu(_dot(xb, wg_ref[...])) * _dot(xb, wu_ref[...])
    acc_ref[...] += _dot(h.astype(BF16), wd_ref[...])

    @pl.when(f == pl.num_programs(1) - 1)
    def _():
        o_ref[...] = _layer_norm(ALPHA * x_ref[...] + acc_ref[...], lg_ref[...], lb_ref[...])


def _ffn(x, w_gate, w_up, w_down, ln_g, ln_b, S):
    T = x.shape[0]
    t = _tiles(S)
    tm, tf = t["tm_in"], t["tf"]
    return pl.pallas_call(
        _ffn_kernel,
        grid=(T // tm, D_FF_PAD // tf),
        in_specs=[
            pl.BlockSpec((tm, D_MODEL), lambda i, f: (i, 0)),
            pl.BlockSpec((D_MODEL, tf), lambda i, f: (0, f)),
            pl.BlockSpec((D_MODEL, tf), lambda i, f: (0, f)),
            pl.BlockSpec((tf, D_MODEL), lambda i, f: (f, 0)),
            pl.BlockSpec((1, D_MODEL), lambda i, f: (0, 0)),
            pl.BlockSpec((1, D_MODEL), lambda i, f: (0, 0)),
        ],
        out_specs=pl.BlockSpec((tm, D_MODEL), lambda i, f: (i, 0)),
        out_shape=jax.ShapeDtypeStruct((T, D_MODEL), F32),
        scratch_shapes=[pltpu.VMEM((tm, D_MODEL), BF16), pltpu.VMEM((tm, D_MODEL), F32)],
        compiler_params=pltpu.CompilerParams(
            dimension_semantics=("parallel", "arbitrary"), vmem_limit_bytes=56 * MIB),
        name="ffn_dense",
    )(x, w_gate, w_up, w_down, ln_g, ln_b)


def _route_tables(route, counts, T):
    R = MOE_ROWS
    as_int = lambda n: route[:, n].astype(jnp.int32)
    i1, i2, r1, r2 = as_int(2), as_int(3), as_int(4), as_int(5)
    cnt = counts[0, :N_EXPERTS].astype(jnp.int32)
    padded = (cnt + R - 1) // R * R
    ends = jnp.cumsum(padded)
    off = ends - padded
    n_tiles = TOP_K * T // R + N_EXPERTS
    tile_expert = jnp.minimum(
        jnp.searchsorted(ends, jnp.arange(n_tiles, dtype=jnp.int32) * R, side="right"),
        N_EXPERTS - 1).astype(jnp.int32)
    pad_start = off + cnt
    pad_n = (padded - cnt).at[N_EXPERTS - 1].set(n_tiles * R - pad_start[N_EXPERTS - 1])
    return dict(
        pos1=jnp.take(off, i1) + r1, pos2=jnp.take(off, i2) + r2,
        tile_expert=tile_expert, n_used=(ends[-1:] // R).astype(jnp.int32),
        pad=jnp.concatenate([pad_start, pad_n]).astype(jnp.int32), n_tiles=n_tiles)


def _dispatch_kernel(pad_ref, x_ref, p1_ref, p2_ref, xs_ref, zero_ref, sem):
    tm = x_ref.shape[0]

    def row_copy(src_ref, src_row, dst_row):
        return pltpu.make_async_copy(src_ref.at[pl.ds(src_row, 1)], xs_ref.at[pl.ds(dst_row, 1)], sem)

    @pl.when(pl.program_id(0) == 0)
    def _():
        zero_ref[...] = jnp.zeros_like(zero_ref)
        for e in range(N_EXPERTS):
            start = pad_ref[e]
            n = pad_ref[N_EXPERTS + e]

            def issue_pad(k, c):
                row_copy(zero_ref, 0, start + k).start()
                return c

            def drain_pad(k, c):
                row_copy(zero_ref, 0, start + k).wait()
                return c

            lax.fori_loop(0, n, issue_pad, 0)
            lax.fori_loop(0, n, drain_pad, 0)

    def issue(t, c):
        row_copy(x_ref, t, p1_ref[0, 0, t]).start(priority=0)
        row_copy(x_ref, t, p2_ref[0, 0, t]).start(priority=1)
        return c

    def drain(t, c):
        row_copy(x_ref, t, p1_ref[0, 0, t]).wait()
        row_copy(x_ref, t, p2_ref[0, 0, t]).wait()
        return c

    lax.fori_loop(0, tm, issue, 0, unroll=8)
    lax.fori_loop(0, tm, drain, 0, unroll=8)


def _dispatch(x, tabs, S):
    T = x.shape[0]
    tm = _tiles(S)["tm_in"]
    n_rows = tabs["n_tiles"] * MOE_ROWS
    idx_spec = pl.BlockSpec((1, 1, tm), lambda i, pad: (i, 0, 0), memory_space=pltpu.SMEM)
    return pl.pallas_call(
        _dispatch_kernel,
        grid_spec=pltpu.PrefetchScalarGridSpec(
            num_scalar_prefetch=1,
            grid=(T // tm,),
            in_specs=[pl.BlockSpec((tm, D_MODEL), lambda i, pad: (i, 0)), idx_spec, idx_spec],
            out_specs=pl.BlockSpec(memory_space=pl.ANY),
            scratch_shapes=[pltpu.VMEM((SUBLANES, D_MODEL), F32), pltpu.SemaphoreType.DMA(())],
        ),
        out_shape=jax.ShapeDtypeStruct((n_rows, D_MODEL), F32),
        compiler_params=pltpu.CompilerParams(dimension_semantics=("arbitrary",)),
        name="moe_dispatch",
    )(tabs["pad"], x, tabs["pos1"].reshape(T // tm, 1, tm), tabs["pos2"].reshape(T // tm, 1, tm))


def _gmm_kernel(te_ref, nu_ref, x_ref, wg_ref, wu_ref, wd_ref, o_ref, *, tf):
    used = pl.program_id(0) < nu_ref[0]

    @pl.when(jnp.logical_not(used))
    def _():
        o_ref[...] = jnp.zeros_like(o_ref)

    @pl.when(used)
    def _():
        xb = x_ref[...].astype(BF16)
        y = None
        for f in range(D_FF_PAD // tf):
            cols = slice(f * tf, (f + 1) * tf)
            h = jax.nn.silu(_dot(xb, wg_ref[0, :, cols])) * _dot(xb, wu_ref[0, :, cols])
            part = _dot(h.astype(BF16), wd_ref[0, cols, :])
            y = part if y is None else y + part
        o_ref[...] = y


def _gmm(xs, tabs, w_gate, w_up, w_down, S):
    R = MOE_ROWS
    tf = _tiles(S)["tf"]
    row_spec = pl.BlockSpec((R, D_MODEL), lambda j, te, nu: (j, 0))
    w_spec = lambda shape: pl.BlockSpec(shape, lambda j, te, nu: (te[j], 0, 0),
                                        pipeline_mode=pl.Buffered(1))
    return pl.pallas_call(
        functools.partial(_gmm_kernel, tf=tf),
        grid_spec=pltpu.PrefetchScalarGridSpec(
            num_scalar_prefetch=2,
            grid=(tabs["n_tiles"],),
            in_specs=[row_spec, w_spec((1, D_MODEL, D_FF_PAD)), w_spec((1, D_MODEL, D_FF_PAD)),
                      w_spec((1, D_FF_PAD, D_MODEL))],
            out_specs=row_spec,
        ),
        out_shape=jax.ShapeDtypeStruct(xs.shape, F32),
        compiler_params=pltpu.CompilerParams(
            dimension_semantics=("arbitrary",), vmem_limit_bytes=56 * MIB),
        name="moe_gmm",
    )(tabs["tile_expert"], tabs["n_used"], xs, w_gate, w_up, w_down)


def _combine_kernel(x_ref, route_ref, p1_ref, p2_ref, lg_ref, lb_ref, ys_ref, o_ref, y1_ref, y2_ref, sem):
    tm = x_ref.shape[0]

    def row_copies(t):
        return (pltpu.make_async_copy(ys_ref.at[pl.ds(p1_ref[0, 0, t], 1)], y1_ref.at[pl.ds(t, 1)], sem),
                pltpu.make_async_copy(ys_ref.at[pl.ds(p2_ref[0, 0, t], 1)], y2_ref.at[pl.ds(t, 1)], sem))

    def issue(t, c):
        for priority, cp in enumerate(row_copies(t)):
            cp.start(priority=priority)
        return c

    def drain(t, c):
        for cp in row_copies(t):
            cp.wait()
        return c

    lax.fori_loop(0, tm, issue, 0, unroll=8)
    lax.fori_loop(0, tm, drain, 0, unroll=8)
    route = route_ref[...]
    ff = route[:, 0:1] * y1_ref[...] + route[:, 1:2] * y2_ref[...]
    o_ref[...] = _layer_norm(ALPHA * x_ref[...] + ff, lg_ref[...], lb_ref[...])


def _combine(x, route, tabs, ys, ln_g, ln_b, S):
    T = x.shape[0]
    tm = _tiles(S)["tm_comb"]
    idx_spec = pl.BlockSpec((1, 1, tm), lambda i: (i, 0, 0), memory_space=pltpu.SMEM)
    return pl.pallas_call(
        _combine_kernel,
        grid=(T // tm,),
        in_specs=[
            pl.BlockSpec((tm, D_MODEL), lambda i: (i, 0)),
            pl.BlockSpec((tm, ROUTER_PAD), lambda i: (i, 0)),
            idx_spec, idx_spec,
            pl.BlockSpec((1, D_MODEL), lambda i: (0, 0)),
            pl.BlockSpec((1, D_MODEL), lambda i: (0, 0)),
            pl.BlockSpec(memory_space=pl.ANY),
        ],
        out_specs=pl.BlockSpec((tm, D_MODEL), lambda i: (i, 0)),
        out_shape=jax.ShapeDtypeStruct((T, D_MODEL), F32),
        scratch_shapes=[pltpu.VMEM((tm, D_MODEL), F32), pltpu.VMEM((tm, D_MODEL), F32),
                        pltpu.SemaphoreType.DMA(())],
        compiler_params=pltpu.CompilerParams(dimension_semantics=("arbitrary",)),
        name="moe_combine",
    )(x, route, tabs["pos1"].reshape(T // tm, 1, tm), tabs["pos2"].reshape(T // tm, 1, tm), ln_g, ln_b, ys)


def _rope_tables(S):
    half = DIFF_HEAD_DIM // 2
    inv = ROPE_THETA ** (-jnp.arange(0, DIFF_HEAD_DIM, 2, dtype=F32) / DIFF_HEAD_DIM)
    ang = jnp.arange(S, dtype=F32)[:, None] * inv[None, :]
    cos, sin = jnp.cos(ang), jnp.sin(ang)
    reps = LANES // DIFF_HEAD_DIM
    cos_t = jnp.tile(jnp.concatenate([cos, cos], axis=1), (1, reps))
    sin_t = jnp.tile(jnp.concatenate([-sin, sin], axis=1), (1, reps))
    assert cos_t.shape == (S, LANES) and half * 2 == DIFF_HEAD_DIM
    return cos_t, sin_t


def _dft_tables(S):
    split = math.gcd(S, DFT_ROW_SPLIT)
    k = jnp.arange(S, dtype=jnp.int32)[None, :]

    def thin(rows):
        ang = ((rows[:, None] * k) % S).astype(F32) * (2.0 * math.pi / S)
        return jnp.cos(ang), jnp.sin(ang)

    ch, sh = thin(jnp.arange(S // split, dtype=jnp.int32) * split)
    cl, sl = thin(jnp.arange(split, dtype=jnp.int32))
    scale = S ** -0.5
    cos = (ch[:, None, :] * cl[None, :, :] - sh[:, None, :] * sl[None, :, :]) * scale
    sin = (sh[:, None, :] * cl[None, :, :] + ch[:, None, :] * sl[None, :, :]) * -scale
    return cos.reshape(S, S).astype(BF16), sin.reshape(S, S).astype(BF16)


def _dft64_table():
    j = jnp.arange(FOURIER_GROUP, dtype=jnp.int32)
    ang = ((j[:, None] * j[None, :]) % FOURIER_GROUP).astype(F32) * (2.0 * math.pi / FOURIER_GROUP)
    scale = FOURIER_GROUP ** -0.5
    eye = jnp.eye(N_FOURIER_GROUPS, dtype=F32)
    return jnp.concatenate([jnp.kron(eye, jnp.cos(ang) * scale),
                            jnp.kron(eye, jnp.sin(ang) * scale)], axis=1).astype(BF16)


def _pad_ff(w, axis):
    pad = [(0, 0)] * w.ndim
    pad[axis] = (0, D_FF_PAD - D_FF)
    return jnp.pad(w, pad).astype(BF16)


def _split_router(w_router):
    w = jnp.pad(w_router.astype(F32), ((0, 0), (0, 0), (0, ROUTER_PAD - N_EXPERTS)))
    hi = w.astype(BF16)
    lo = (w - hi.astype(F32)).astype(BF16)
    return jnp.concatenate([hi, hi, lo], axis=1)


def _prepare(w_in,w_fourier, w_sgu, w_diff, w_out, vn_g, vn_b, sgu_w, sgu_b,
             lam_q1, lam_k1, lam_q2, lam_k2, subln_g, ln1_g, ln1_b, ln2_g, ln2_b,
             ffn_w_gate, ffn_w_up, ffn_w_down, w_router, moe_w_gate, moe_w_up, moe_w_down):
    w_in_b = w_in.astype(BF16)
    row = lambda a: a.astype(F32)[:, None, :]
    return dict(
        w_a=w_in_b[:, :, :OFF_G], w_g=w_in_b[:, :, OFF_G:],
        w_f=w_fourier.astype(BF16), w_s=w_sgu.astype(BF16), w_d=w_diff.astype(BF16),
        w_o=w_out.astype(BF16),
        vn_g=row(vn_g), vn_b=row(vn_b),
        sgu_wcat=jnp.transpose(sgu_w, (0, 2, 1, 3)).reshape(DEPTH, CHUNK, SGU_HEADS * CHUNK).astype(BF16),
        sgu_bias=jnp.repeat(jnp.transpose(sgu_b, (0, 2, 1)).astype(F32), SGU_HEAD_DIM, axis=2),
        lam=jnp.stack([lam_q1, lam_k1, lam_q2, lam_k2], axis=1).astype(F32),
        subln_g=row(subln_g), ln1_g=row(ln1_g), ln1_b=row(ln1_b), ln2_g=row(ln2_g), ln2_b=row(ln2_b),
        ffn_gate=_pad_ff(ffn_w_gate, 2), ffn_up=_pad_ff(ffn_w_up, 2), ffn_down=_pad_ff(ffn_w_down, 1),
        w_router=_split_router(w_router),
        moe_gate=_pad_ff(moe_w_gate, 3), moe_up=_pad_ff(moe_w_up, 3), moe_down=_pad_ff(moe_w_down, 2),
        dft64=_dft64_table(),
    )


def _trunk(x3, p):
    B, S, _ = x3.shape
    x = x3.reshape(B * S, D_MODEL)
    rope_cos, rope_sin = _rope_tables(S)
    ct, snt = _dft_tables(S)
    for l in range(DEPTH):
        lambda_init = 0.8 - 0.6 * math.exp(-0.3 * l)
        g, so, qt, k, vat = _inproj(x, p["w_a"][l], p["dft64"], p["vn_g"][l], p["vn_b"][l],
                                    p["sgu_wcat"][l], p["sgu_bias"][l], rope_cos, rope_sin, S)
        fo = _fourier(g, ct, snt, B, S)
        do = _attn(p["lam"][l], qt, k, vat, p["subln_g"][l], B, S, lambda_init)
        j = l // 2
        moe = l % 2 == 1
        x1, route, counts = _merge(x, fo, so, do, p["w_g"][l], p["w_f"][l], p["w_s"][l], p["w_d"][l],
                                   p["w_o"][l], p["ln1_g"][l], p["ln1_b"][l],
                                   p["w_router"][j] if moe else None, S)
        if moe:
            tabs = _route_tables(route, counts, B * S)
            xs = _dispatch(x1, tabs, S)
            ys = _gmm(xs, tabs, p["moe_gate"][j], p["moe_up"][j], p["moe_down"][j], S)
            x = _combine(x1, route, tabs, ys, p["ln2_g"][l], p["ln2_b"][l], S)
        else:
            x = _ffn(x1, p["ffn_gate"][j], p["ffn_up"][j], p["ffn_down"][j],
                     p["ln2_g"][l], p["ln2_b"][l], S)
    return x.reshape(B, S, D_MODEL)


def kernel(x_prompt, x_sample, w_in, w_fourier, w_sgu, w_diff, w_out, vn_g, vn_b, sgu_w, sgu_b,
           lam_q1, lam_k1, lam_q2, lam_k2, subln_g, ln1_g, ln1_b, ln2_g, ln2_b,
           ffn_w_gate, ffn_w_up, ffn_w_down, w_router, moe_w_gate, moe_w_up, moe_w_down):
    p = _prepare(w_in, w_fourier, w_sgu, w_diff, w_out, vn_g, vn_b, sgu_w, sgu_b,
                 lam_q1, lam_k1, lam_q2, lam_k2, subln_g, ln1_g, ln1_b, ln2_g, ln2_b,
                 ffn_w_gate, ffn_w_up, ffn_w_down, w_router, moe_w_gate, moe_w_up, moe_w_down)
    return (_trunk(x_prompt, p), _trunk(x_sample, p))
```

```python
import functools
import math

import jax
import jax.numpy as jnp
from jax import lax
from jax.experimental import pallas as pl
from jax.experimental.pallas import tpu as pltpu

F32 = jnp.float32
BF16 = jnp.bfloat16

D_MODEL = 1024
DEPTH = 2
N_FOURIER_GROUPS = 4
FOURIER_GROUP = 64
FOURIER_WIDTH = N_FOURIER_GROUPS * FOURIER_GROUP
SGU_HEADS = 4
SGU_HEAD_DIM = 64
SGU_WIDTH = SGU_HEADS * SGU_HEAD_DIM
CHUNK = 128
DIFF_HEADS = 4
DIFF_HEAD_DIM = 64
DIFF_V_DIM = 2 * DIFF_HEAD_DIM
DIFF_QK_WIDTH = DIFF_HEADS * 2 * DIFF_HEAD_DIM
DIFF_V_WIDTH = DIFF_HEADS * DIFF_V_DIM
ROPE_THETA = 10000.0
N_BRANCHES = 3
D_FF = 2752
N_EXPERTS = 8
ALPHA = (2 * DEPTH) ** 0.25
LN_EPS = 1e-5
RMS_EPS = 1e-5

OFF_F = 0
OFF_U = OFF_F + FOURIER_WIDTH
OFF_V = OFF_U + SGU_WIDTH
OFF_Q = OFF_V + SGU_WIDTH
OFF_K = OFF_Q + DIFF_QK_WIDTH
OFF_VA = OFF_K + DIFF_QK_WIDTH
OFF_G = OFF_VA + DIFF_V_WIDTH
IN_WIDTH = OFF_G + N_BRANCHES * D_MODEL

TOP_K = 2
LANES = 128
SUBLANES = 8
MXU_WIDTH = 256
MOE_ROWS = 512
DFT_ROW_SPLIT = 64
SCORE_LEAD = 4
D_FF_PAD = 2816
ROUTER_PAD = LANES
MIB = 1024 * 1024


def _tiles(S):
    t = dict(
        tm_in=min(512, S),
        tm_comb=min(256, S),
        tm_dft=min(512, S),
        tq=min(512, S),
        tkc=min(128, S),
        tkp=min(512, S),
        tf=D_FF_PAD // 2,
    )
    return t


def _layer_norm(z, g, b):
    mu = jnp.mean(z, axis=-1, keepdims=True)
    zc = z - mu
    var = jnp.mean(zc * zc, axis=-1, keepdims=True)
    return zc * lax.rsqrt(var + LN_EPS) * g + b


def _dot(a, b):
    return jnp.dot(a, b, preferred_element_type=F32)


def _inproj_kernel(x_ref, w_ref, dft_ref, vng_ref, vnb_ref, sguw_ref, sgub_ref, cos_ref, sin_ref,
                   g_out, so_out, qt_out, k_out, vat_out):
    tm = x_ref.shape[0]
    xb = x_ref[...].astype(BF16)

    def proj(a, b):
        return _dot(xb, w_ref[:, a:b])

    f = proj(OFF_F, OFF_U)
    g_out[...] = _dot(f.astype(BF16), dft_ref[...]).astype(BF16)

    u = proj(OFF_U, OFF_V)
    v = proj(OFF_V, OFF_Q)
    vn = _layer_norm(v, vng_ref[...], vnb_ref[...]).astype(BF16)
    row_head = lax.broadcasted_iota(jnp.int32, (SGU_HEADS * CHUNK, SGU_WIDTH), 0) // CHUNK
    col_head = lax.broadcasted_iota(jnp.int32, (SGU_HEADS * CHUNK, SGU_WIDTH), 1) // SGU_HEAD_DIM
    head_mask = row_head == col_head
    for c in range(tm // CHUNK):
        rows = slice(c * CHUNK, (c + 1) * CHUNK)
        vb = vn[rows, :]
        rhs = jnp.where(head_mask, jnp.concatenate([vb] * SGU_HEADS, axis=0), jnp.zeros((), BF16))
        mixed = _dot(sguw_ref[...], rhs) + sgub_ref[...]
        so_out[rows, :] = (u[rows, :] * mixed).astype(BF16)

    cs = cos_ref[...]
    sn = sin_ref[...]
    lane = lax.broadcasted_iota(jnp.int32, (tm, LANES), 1)
    first_half = (lane % DIFF_HEAD_DIM) < (DIFF_HEAD_DIM // 2)

    def rope(t):
        outs = []
        for j in range(DIFF_QK_WIDTH // LANES):
            tc = t[:, j * LANES:(j + 1) * LANES]
            rot = jnp.where(first_half,
                            pltpu.roll(tc, LANES - DIFF_HEAD_DIM // 2, 1),
                            pltpu.roll(tc, DIFF_HEAD_DIM // 2, 1))
            outs.append(tc * cs + rot * sn)
        return jnp.concatenate(outs, axis=1)

    q = rope(proj(OFF_Q, OFF_K)) * (DIFF_HEAD_DIM ** -0.5)
    qt_out[...] = q.T.astype(BF16)
    k_out[...] = rope(proj(OFF_K, OFF_VA)).astype(BF16)
    vat_out[...] = proj(OFF_VA, OFF_G).T.astype(BF16)


def _inproj(x, w_a, dft64, vn_g, vn_b, sgu_wcat, sgu_bias, rope_cos, rope_sin, S):
    T = x.shape[0]
    tm = _tiles(S)["tm_in"]
    n_pos = S // tm
    full = lambda shape: pl.BlockSpec(shape, lambda i: (0,) * len(shape))
    return pl.pallas_call(
        _inproj_kernel,
        grid=(T // tm,),
        in_specs=[
            pl.BlockSpec((tm, D_MODEL), lambda i: (i, 0)),
            full((D_MODEL, OFF_G)),
            full((FOURIER_WIDTH, 2 * FOURIER_WIDTH)),
            full((1, SGU_WIDTH)),
            full((1, SGU_WIDTH)),
            full((CHUNK, SGU_HEADS * CHUNK)),
            full((CHUNK, SGU_WIDTH)),
            pl.BlockSpec((tm, LANES), lambda i: (i % n_pos, 0)),
            pl.BlockSpec((tm, LANES), lambda i: (i % n_pos, 0)),
        ],
        out_specs=[
            pl.BlockSpec((tm, 2 * FOURIER_WIDTH), lambda i: (i, 0)),
            pl.BlockSpec((tm, SGU_WIDTH), lambda i: (i, 0)),
            pl.BlockSpec((DIFF_QK_WIDTH, tm), lambda i: (0, i)),
            pl.BlockSpec((tm, DIFF_QK_WIDTH), lambda i: (i, 0)),
            pl.BlockSpec((DIFF_V_WIDTH, tm), lambda i: (0, i)),
        ],
        out_shape=[
            jax.ShapeDtypeStruct((T, 2 * FOURIER_WIDTH), BF16),
            jax.ShapeDtypeStruct((T, SGU_WIDTH), BF16),
            jax.ShapeDtypeStruct((DIFF_QK_WIDTH, T), BF16),
            jax.ShapeDtypeStruct((T, DIFF_QK_WIDTH), BF16),
            jax.ShapeDtypeStruct((DIFF_V_WIDTH, T), BF16),
        ],
        compiler_params=pltpu.CompilerParams(
            dimension_semantics=("parallel",), vmem_limit_bytes=48 * MIB),
        name="inproj",
    )(x, w_a, dft64, vn_g, vn_b, sgu_wcat, sgu_bias, rope_cos, rope_sin)


def _fourier_kernel(ct_ref, snt_ref, g_ref, o_ref):
    acc = _dot(ct_ref[...], g_ref[:, :FOURIER_WIDTH])
    acc = acc + _dot(snt_ref[...], g_ref[:, FOURIER_WIDTH:])
    o_ref[...] = acc.astype(BF16)


def _fourier(g, ct, snt, B, S):
    tm = _tiles(S)["tm_dft"]
    n_row = S // tm
    return pl.pallas_call(
        _fourier_kernel,
        grid=(n_row, B),
        in_specs=[
            pl.BlockSpec((tm, S), lambda i, b: (i, 0), pipeline_mode=pl.Buffered(1)),
            pl.BlockSpec((tm, S), lambda i, b: (i, 0), pipeline_mode=pl.Buffered(1)),
            pl.BlockSpec((S, 2 * FOURIER_WIDTH), lambda i, b: (b, 0)),
        ],
        out_specs=pl.BlockSpec((tm, FOURIER_WIDTH), lambda i, b: (b * n_row + i, 0)),
        out_shape=jax.ShapeDtypeStruct((B * S, FOURIER_WIDTH), BF16),
        compiler_params=pltpu.CompilerParams(
            dimension_semantics=("parallel", "parallel"), vmem_limit_bytes=48 * MIB),
        name="fourier",
    )(ct, snt, g)


def _attn_kernel(lam_ref, qt_ref, k_ref, vt_ref, g_ref, o_ref, s_ref, *, tkp, lambda_init):
    tq = qt_ref.shape[1]
    width = 2 * tq
    S = k_ref.shape[0]
    n_p = S // tkp
    groups = tkp // SUBLANES
    ring = s_ref.shape[0]
    lp = lam_ref[...]
    lam = (jnp.exp(jnp.sum(lp[0:1, :] * lp[1:2, :], axis=-1, keepdims=True))
           - jnp.exp(jnp.sum(lp[2:3, :] * lp[3:4, :], axis=-1, keepdims=True))
           + lambda_init)
    qt = qt_ref[...]
    row = lax.broadcasted_iota(jnp.int32, qt.shape, 0)
    zero = jnp.zeros((), BF16)
    qt_both = jnp.concatenate(
        [jnp.where(row < DIFF_HEAD_DIM, qt, zero), jnp.where(row >= DIFF_HEAD_DIM, qt, zero)], axis=1)

    m_class = jnp.full((SUBLANES, width), -jnp.inf, F32)
    m_exp = jnp.full((1, width), -jnp.inf, F32)
    l_class = jnp.zeros((SUBLANES, width), F32)
    acc = jnp.zeros((DIFF_V_DIM, width), F32)
    m_hist = []
    l_hist = []
    for p in range(n_p + 1):
        if p < n_p:
            hold = 0
            if p >= ring:
                bits = pltpu.bitcast(l_hist[p - ring], jnp.int32)[0, 0]
                half = jnp.int32(16)
                hold = lax.shift_right_logical(lax.shift_right_logical(bits, half), half)
            kc = k_ref[pl.ds(pl.multiple_of(p * tkp + hold, tkp), tkp), :]
            s = _dot(kc, qt_both)
            s_ref[p % ring] = s
            m_class = jnp.maximum(m_class, jnp.max(s.reshape(groups, SUBLANES, width), axis=0))
            m_hist.append(jnp.max(m_class, axis=0, keepdims=True))
        if p >= 1:
            j = p - 1
            m_new = m_hist[j]
            alpha = jnp.exp(m_exp - m_new)
            m_rows = jnp.broadcast_to(m_new, (SUBLANES, width))
            vt_chunk = vt_ref[:, j * tkp:(j + 1) * tkp]
            l_parts, acc_parts = [], []
            for c0 in range(0, width, MXU_WIDTH):
                cols = slice(c0, c0 + MXU_WIDTH)
                e = jnp.exp(s_ref[j % ring, :, cols].reshape(groups, SUBLANES, MXU_WIDTH)
                            - m_rows[None, :, cols])
                l_parts.append(l_class[:, cols] * alpha[:, cols] + jnp.sum(e, axis=0))
                acc_parts.append(acc[:, cols] * alpha[:, cols]
                                 + _dot(vt_chunk, e.reshape(tkp, MXU_WIDTH).astype(BF16)))
            l_class = jnp.concatenate(l_parts, axis=1)
            acc = jnp.concatenate(acc_parts, axis=1)
            m_exp = m_new
            l_hist.append(l_class)

    l = jnp.sum(l_class, axis=0, keepdims=True)
    o_t = acc[:, :tq] / l[:, :tq] - lam * (acc[:, tq:] / l[:, tq:])
    o = o_t.T
    o = o * lax.rsqrt(jnp.mean(o * o, axis=-1, keepdims=True) + RMS_EPS) * g_ref[...]
    o_ref[...] = (o * (1.0 - lambda_init)).astype(BF16)


def _attn(lam_params, qt, k, vat, subln_g, B, S, lambda_init):
    t = _tiles(S)
    tq, tkp = t["tq"], t["tkp"]
    n_q = S // tq
    kern = functools.partial(_attn_kernel, tkp=tkp, lambda_init=lambda_init)
    return pl.pallas_call(
        kern,
        grid=(B, DIFF_HEADS, n_q),
        in_specs=[
            pl.BlockSpec((4, DIFF_HEAD_DIM), lambda b, h, i: (0, 0)),
            pl.BlockSpec((DIFF_V_DIM, tq), lambda b, h, i: (h, b * n_q + i)),
            pl.BlockSpec((S, DIFF_V_DIM), lambda b, h, i: (b, h)),
            pl.BlockSpec((DIFF_V_DIM, S), lambda b, h, i: (h, b)),
            pl.BlockSpec((1, DIFF_V_DIM), lambda b, h, i: (0, 0)),
        ],
        out_specs=pl.BlockSpec((tq, DIFF_V_DIM), lambda b, h, i: (b * n_q + i, h)),
        out_shape=jax.ShapeDtypeStruct((B * S, DIFF_V_WIDTH), BF16),
        scratch_shapes=[pltpu.VMEM((min(SCORE_LEAD, S // tkp), tkp, 2 * tq), F32)],
        compiler_params=pltpu.CompilerParams(
            dimension_semantics=("parallel", "parallel", "parallel"), vmem_limit_bytes=48 * MIB),
        name="attn",
    )(lam_params, qt, k, vat, subln_g)


def _merge_kernel(*refs, with_router):
    if with_router:
        (x_ref, fo_ref, so_ref, do_ref, wg_ref, wf_ref, ws_ref, wd_ref, wo_ref, lg_ref, lb_ref,
         wr_ref, x1_out, route_out, counts_out, base_ref) = refs
    else:
        (x_ref, fo_ref, so_ref, do_ref, wg_ref, wf_ref, ws_ref, wd_ref, wo_ref, lg_ref, lb_ref,
         x1_out) = refs
    x = x_ref[...]
    xb = x.astype(BF16)
    merged = None
    for j, (br_ref, wb_ref) in enumerate(((fo_ref, wf_ref), (so_ref, ws_ref), (do_ref, wd_ref))):
        gate = jax.nn.sigmoid(_dot(xb, wg_ref[:, j * D_MODEL:(j + 1) * D_MODEL]))
        term = gate * _dot(br_ref[...], wb_ref[...])
        merged = term if merged is None else merged + term
    y = _dot(merged.astype(BF16), wo_ref[...])
    x1 = _layer_norm(ALPHA * x + y, lg_ref[...], lb_ref[...])
    x1_out[...] = x1

    if with_router:
        tm = x.shape[0]
        logits = _dot(x1.astype(BF16), wr_ref[...])
        lane = lax.broadcasted_iota(jnp.int32, (tm, ROUTER_PAD), 1)
        neg = jnp.float32(-jnp.inf)
        lg = jnp.where(lane < N_EXPERTS, logits, neg)
        v1 = jnp.max(lg, axis=-1, keepdims=True)
        i1 = jnp.min(jnp.where(lg == v1, lane, ROUTER_PAD), axis=-1, keepdims=True)
        lg2 = jnp.where(lane == i1, neg, lg)
        v2 = jnp.max(lg2, axis=-1, keepdims=True)
        i2 = jnp.min(jnp.where(lg2 == v2, lane, ROUTER_PAD), axis=-1, keepdims=True)
        e2 = jnp.exp(v2 - v1)
        den = 1.0 + e2

        @pl.when(pl.program_id(0) == 0)
        def _():
            base_ref[...] = jnp.zeros_like(base_ref)

        pick1 = lane == i1
        pick2 = lane == i2
        sel = jnp.logical_or(pick1, pick2)
        r_id = lax.broadcasted_iota(jnp.int32, (tm, tm), 0)
        c_id = lax.broadcasted_iota(jnp.int32, (tm, tm), 1)
        ltri = (c_id < r_id).astype(BF16)
        base = base_ref[0:1, :]
        rank = base + _dot(ltri, sel.astype(BF16))
        r1 = jnp.sum(jnp.where(pick1, rank, 0.0), axis=-1, keepdims=True)
        r2 = jnp.sum(jnp.where(pick2, rank, 0.0), axis=-1, keepdims=True)
        new_base = base + jnp.sum(sel.astype(F32), axis=0, keepdims=True)
        base_ref[...] = jnp.broadcast_to(new_base, base_ref.shape)
        counts_out[...] = jnp.broadcast_to(new_base, counts_out.shape)
        fields = (1.0 / den, e2 / den, i1.astype(F32), i2.astype(F32), r1, r2)
        route = jnp.zeros((tm, ROUTER_PAD), F32)
        for n, val in enumerate(fields):
            route = jnp.where(lane == n, val, route)
        route_out[...] = route


def _merge(x, fo, so, do, w_g, w_f, w_s, w_d, w_o, ln_g, ln_b, w_router, S):
    T = x.shape[0]
    tm = _tiles(S)["tm_in"]
    with_router = w_router is not None
    full = lambda shape: pl.BlockSpec(shape, lambda i: (0,) * len(shape))
    row = lambda width: pl.BlockSpec((tm, width), lambda i: (i, 0))
    in_specs = [
        row(D_MODEL), row(FOURIER_WIDTH), row(SGU_WIDTH), row(DIFF_V_WIDTH),
        full((D_MODEL, N_BRANCHES * D_MODEL)),
        full((FOURIER_WIDTH, D_MODEL)), full((SGU_WIDTH, D_MODEL)), full((DIFF_V_WIDTH, D_MODEL)),
        full((D_MODEL, D_MODEL)), full((1, D_MODEL)), full((1, D_MODEL)),
    ]
    args = [x, fo, so, do, w_g, w_f, w_s, w_d, w_o, ln_g, ln_b]
    out_specs = [row(D_MODEL)]
    out_shape = [jax.ShapeDtypeStruct((T, D_MODEL), F32)]
    scratch = []
    if with_router:
        in_specs.append(full((D_MODEL, ROUTER_PAD)))
        args.append(w_router)
        out_specs += [row(ROUTER_PAD), full((SUBLANES, ROUTER_PAD))]
        out_shape += [jax.ShapeDtypeStruct((T, ROUTER_PAD), F32),
                      jax.ShapeDtypeStruct((SUBLANES, ROUTER_PAD), F32)]
        scratch.append(pltpu.VMEM((SUBLANES, ROUTER_PAD), F32))
    outs = pl.pallas_call(
        functools.partial(_merge_kernel, with_router=with_router),
        grid=(T // tm,),
        in_specs=in_specs,
        out_specs=out_specs,
        out_shape=out_shape,
        scratch_shapes=scratch,
        compiler_params=pltpu.CompilerParams(
            dimension_semantics=("arbitrary" if with_router else "parallel",),
            vmem_limit_bytes=56 * MIB),
        name="merge_router" if with_router else "merge",
    )(*args)
    return (outs[0], outs[1], outs[2]) if with_router else (outs[0], None, None)


def _swiglu(xb, wg_ref, wu_ref, wd_ref, tf):
    y = None
    for f in range(D_FF_PAD // tf):
        cols = slice(f * tf, (f + 1) * tf)
        h = jax.nn.silu(_dot(xb, wg_ref[:, cols])) * _dot(xb, wu_ref[:, cols])
        part = _dot(h.astype(BF16), wd_ref[cols, :])
        y = part if y is None else y + part
    return y


def _ffn_kernel(x_ref, wg_ref, wu_ref, wd_ref, lg_ref, lb_ref, o_ref, *, tf):
    x = x_ref[...]
    ff = _swiglu(x.astype(BF16), wg_ref, wu_ref, wd_ref, tf)
    o_ref[...] = _layer_norm(ALPHA * x + ff, lg_ref[...], lb_ref[...])


def _ffn(x, w_gate, w_up, w_down, ln_g, ln_b, S):
    T = x.shape[0]
    t = _tiles(S)
    tm, tf = t["tm_in"], t["tf"]
    resident = lambda shape: pl.BlockSpec(shape, lambda i: (0, 0), pipeline_mode=pl.Buffered(1))
    return pl.pallas_call(
        functools.partial(_ffn_kernel, tf=tf),
        grid=(T // tm,),
        in_specs=[
            pl.BlockSpec((tm, D_MODEL), lambda i: (i, 0)),
            resident((D_MODEL, D_FF_PAD)), resident((D_MODEL, D_FF_PAD)), resident((D_FF_PAD, D_MODEL)),
            pl.BlockSpec((1, D_MODEL), lambda i: (0, 0)),
            pl.BlockSpec((1, D_MODEL), lambda i: (0, 0)),
        ],
        out_specs=pl.BlockSpec((tm, D_MODEL), lambda i: (i, 0)),
        out_shape=jax.ShapeDtypeStruct((T, D_MODEL), F32),
        compiler_params=pltpu.CompilerParams(
            dimension_semantics=("parallel",), vmem_limit_bytes=56 * MIB),
        name="ffn_dense",
    )(x, w_gate, w_up, w_down, ln_g, ln_b)


def _route_tables(route, counts, T):
    R = MOE_ROWS
    as_int = lambda n: route[:, n].astype(jnp.int32)
    i1, i2, r1, r2 = as_int(2), as_int(3), as_int(4), as_int(5)
    cnt = counts[0, :N_EXPERTS].astype(jnp.int32)
    padded = (cnt + R - 1) // R * R
    ends = jnp.cumsum(padded)
    off = ends - padded
    n_tiles = TOP_K * T // R + N_EXPERTS
    tile_expert = jnp.minimum(
        jnp.searchsorted(ends, jnp.arange(n_tiles, dtype=jnp.int32) * R, side="right"),
        N_EXPERTS - 1).astype(jnp.int32)
    pad_start = off + cnt
    pad_n = (padded - cnt).at[N_EXPERTS - 1].set(n_tiles * R - pad_start[N_EXPERTS - 1])
    return dict(
        pos1=jnp.take(off, i1) + r1, pos2=jnp.take(off, i2) + r2,
        tile_expert=tile_expert, n_used=(ends[-1:] // R).astype(jnp.int32),
        pad=jnp.concatenate([pad_start, pad_n]).astype(jnp.int32), n_tiles=n_tiles)


def _dispatch_kernel(pad_ref, x_ref, p1_ref, p2_ref, xs_ref, zero_ref, sem):
    tm = x_ref.shape[0]

    def row_copy(src_ref, src_row, dst_row):
        return pltpu.make_async_copy(src_ref.at[pl.ds(src_row, 1)], xs_ref.at[pl.ds(dst_row, 1)], sem)

    @pl.when(pl.program_id(0) == 0)
    def _():
        zero_ref[...] = jnp.zeros_like(zero_ref)
        for e in range(N_EXPERTS):
            start = pad_ref[e]
            n = pad_ref[N_EXPERTS + e]

            def issue_pad(k, c):
                row_copy(zero_ref, 0, start + k).start()
                return c

            def drain_pad(k, c):
                row_copy(zero_ref, 0, start + k).wait()
                return c

            lax.fori_loop(0, n, issue_pad, 0)
            lax.fori_loop(0, n, drain_pad, 0)

    def issue(t, c):
        row_copy(x_ref, t, p1_ref[0, 0, t]).start(priority=0)
        row_copy(x_ref, t, p2_ref[0, 0, t]).start(priority=1)
        return c

    def drain(t, c):
        row_copy(x_ref, t, p1_ref[0, 0, t]).wait()
        row_copy(x_ref, t, p2_ref[0, 0, t]).wait()
        return c

    lax.fori_loop(0, tm, issue, 0, unroll=8)
    lax.fori_loop(0, tm, drain, 0, unroll=8)


def _dispatch(x, tabs, S):
    T = x.shape[0]
    tm = _tiles(S)["tm_in"]
    n_rows = tabs["n_tiles"] * MOE_ROWS
    idx_spec = pl.BlockSpec((1, 1, tm), lambda i, pad: (i, 0, 0), memory_space=pltpu.SMEM)
    return pl.pallas_call(
        _dispatch_kernel,
        grid_spec=pltpu.PrefetchScalarGridSpec(
            num_scalar_prefetch=1,
            grid=(T // tm,),
            in_specs=[pl.BlockSpec((tm, D_MODEL), lambda i, pad: (i, 0)), idx_spec, idx_spec],
            out_specs=pl.BlockSpec(memory_space=pl.ANY),
            scratch_shapes=[pltpu.VMEM((SUBLANES, D_MODEL), F32), pltpu.SemaphoreType.DMA(())],
        ),
        out_shape=jax.ShapeDtypeStruct((n_rows, D_MODEL), F32),
        compiler_params=pltpu.CompilerParams(dimension_semantics=("arbitrary",)),
        name="moe_dispatch",
    )(tabs["pad"], x, tabs["pos1"].reshape(T // tm, 1, tm), tabs["pos2"].reshape(T // tm, 1, tm))


def _gmm_kernel(te_ref, nu_ref, x_ref, wg_ref, wu_ref, wd_ref, o_ref, *, tf):
    used = pl.program_id(0) < nu_ref[0]

    @pl.when(jnp.logical_not(used))
    def _():
        o_ref[...] = jnp.zeros_like(o_ref)

    @pl.when(used)
    def _():
        o_ref[...] = _swiglu(x_ref[...].astype(BF16), wg_ref.at[0], wu_ref.at[0], wd_ref.at[0], tf)


def _gmm(xs, tabs, w_gate, w_up, w_down, S):
    R = MOE_ROWS
    tf = _tiles(S)["tf"]
    row_spec = pl.BlockSpec((R, D_MODEL), lambda j, te, nu: (j, 0))
    w_spec = lambda shape: pl.BlockSpec(shape, lambda j, te, nu: (te[j], 0, 0),
                                        pipeline_mode=pl.Buffered(1))
    return pl.pallas_call(
        functools.partial(_gmm_kernel, tf=tf),
        grid_spec=pltpu.PrefetchScalarGridSpec(
            num_scalar_prefetch=2,
            grid=(tabs["n_tiles"],),
            in_specs=[row_spec, w_spec((1, D_MODEL, D_FF_PAD)), w_spec((1, D_MODEL, D_FF_PAD)),
                      w_spec((1, D_FF_PAD, D_MODEL))],
            out_specs=row_spec,
        ),
        out_shape=jax.ShapeDtypeStruct(xs.shape, F32),
        compiler_params=pltpu.CompilerParams(
            dimension_semantics=("arbitrary",), vmem_limit_bytes=56 * MIB),
        name="moe_gmm",
    )(tabs["tile_expert"], tabs["n_used"], xs, w_gate, w_up, w_down)


def _combine_kernel(x_ref, route_ref, p1_ref, p2_ref, lg_ref, lb_ref, ys_ref, o_ref, y1_ref, y2_ref, sem):
    tm = x_ref.shape[0]

    def row_copies(t):
        return (pltpu.make_async_copy(ys_ref.at[pl.ds(p1_ref[0, 0, t], 1)], y1_ref.at[pl.ds(t, 1)], sem),
                pltpu.make_async_copy(ys_ref.at[pl.ds(p2_ref[0, 0, t], 1)], y2_ref.at[pl.ds(t, 1)], sem))

    def issue(t, c):
        for priority, cp in enumerate(row_copies(t)):
            cp.start(priority=priority)
        return c

    def drain(t, c):
        for cp in row_copies(t):
            cp.wait()
        return c

    lax.fori_loop(0, tm, issue, 0, unroll=8)
    lax.fori_loop(0, tm, drain, 0, unroll=8)
    route = route_ref[...]
    ff = route[:, 0:1] * y1_ref[...] + route[:, 1:2] * y2_ref[...]
    o_ref[...] = _layer_norm(ALPHA * x_ref[...] + ff, lg_ref[...], lb_ref[...])


def _combine(x, route, tabs, ys, ln_g, ln_b, S):
    T = x.shape[0]
    tm = _tiles(S)["tm_comb"]
    idx_spec = pl.BlockSpec((1, 1, tm), lambda i: (i, 0, 0), memory_space=pltpu.SMEM)
    return pl.pallas_call(
        _combine_kernel,
        grid=(T // tm,),
        in_specs=[
            pl.BlockSpec((tm, D_MODEL), lambda i: (i, 0)),
            pl.BlockSpec((tm, ROUTER_PAD), lambda i: (i, 0)),
            idx_spec, idx_spec,
            pl.BlockSpec((1, D_MODEL), lambda i: (0, 0)),
            pl.BlockSpec((1, D_MODEL), lambda i: (0, 0)),
            pl.BlockSpec(memory_space=pl.ANY),
        ],
        out_specs=pl.BlockSpec((tm, D_MODEL), lambda i: (i, 0)),
        out_shape=jax.ShapeDtypeStruct((T, D_MODEL), F32),
        scratch_shapes=[pltpu.VMEM((tm, D_MODEL), F32), pltpu.VMEM((tm, D_MODEL), F32),
                        pltpu.SemaphoreType.DMA(())],
        compiler_params=pltpu.CompilerParams(dimension_semantics=("arbitrary",)),
        name="moe_combine",
    )(x, route, tabs["pos1"].reshape(T // tm, 1, tm), tabs["pos2"].reshape(T // tm, 1, tm), ln_g, ln_b, ys)


def _rope_tables(S):
    half = DIFF_HEAD_DIM // 2
    inv = ROPE_THETA ** (-jnp.arange(0, DIFF_HEAD_DIM, 2, dtype=F32) / DIFF_HEAD_DIM)
    ang = jnp.arange(S, dtype=F32)[:, None] * inv[None, :]
    cos, sin = jnp.cos(ang), jnp.sin(ang)
    reps = LANES // DIFF_HEAD_DIM
    cos_t = jnp.tile(jnp.concatenate([cos, cos], axis=1), (1, reps))
    sin_t = jnp.tile(jnp.concatenate([-sin, sin], axis=1), (1, reps))
    assert cos_t.shape == (S, LANES) and half * 2 == DIFF_HEAD_DIM
    return cos_t, sin_t


def _dft_tables(S):
    split = math.gcd(S, DFT_ROW_SPLIT)
    k = jnp.arange(S, dtype=jnp.int32)[None, :]

    def thin(rows):
        ang = ((rows[:, None] * k) % S).astype(F32) * (2.0 * math.pi / S)
        return jnp.cos(ang), jnp.sin(ang)

    ch, sh = thin(jnp.arange(S // split, dtype=jnp.int32) * split)
    cl, sl = thin(jnp.arange(split, dtype=jnp.int32))
    scale = S ** -0.5
    cos = (ch[:, None, :] * cl[None, :, :] - sh[:, None, :] * sl[None, :, :]) * scale
    sin = (sh[:, None, :] * cl[None, :, :] + ch[:, None, :] * sl[None, :, :]) * -scale
    return cos.reshape(S, S).astype(BF16), sin.reshape(S, S).astype(BF16)


def _dft64_table():
    j = jnp.arange(FOURIER_GROUP, dtype=jnp.int32)
    ang = ((j[:, None] * j[None, :]) % FOURIER_GROUP).astype(F32) * (2.0 * math.pi / FOURIER_GROUP)
    scale = FOURIER_GROUP ** -0.5
    eye = jnp.eye(N_FOURIER_GROUPS, dtype=F32)
    return jnp.concatenate([jnp.kron(eye, jnp.cos(ang) * scale),
                            jnp.kron(eye, jnp.sin(ang) * scale)], axis=1).astype(BF16)


def _pad_ff(w, axis):
    pad = [(0, 0)] * w.ndim
    pad[axis] = (0, D_FF_PAD - D_FF)
    return jnp.pad(w, pad).astype(BF16)


def _prepare(w_in,w_fourier, w_sgu, w_diff, w_out, vn_g, vn_b, sgu_w, sgu_b,
             lam_q1, lam_k1, lam_q2, lam_k2, subln_g, ln1_g, ln1_b, ln2_g, ln2_b,
             ffn_w_gate, ffn_w_up, ffn_w_down, w_router, moe_w_gate, moe_w_up, moe_w_down):
    w_in_b = w_in.astype(BF16)
    row = lambda a: a.astype(F32)[:, None, :]
    return dict(
        w_a=w_in_b[:, :, :OFF_G], w_g=w_in_b[:, :, OFF_G:],
        w_f=w_fourier.astype(BF16), w_s=w_sgu.astype(BF16), w_d=w_diff.astype(BF16),
        w_o=w_out.astype(BF16),
        vn_g=row(vn_g), vn_b=row(vn_b),
        sgu_wcat=jnp.transpose(sgu_w, (0, 2, 1, 3)).reshape(DEPTH, CHUNK, SGU_HEADS * CHUNK).astype(BF16),
        sgu_bias=jnp.repeat(jnp.transpose(sgu_b, (0, 2, 1)).astype(F32), SGU_HEAD_DIM, axis=2),
        lam=jnp.stack([lam_q1, lam_k1, lam_q2, lam_k2], axis=1).astype(F32),
        subln_g=row(subln_g), ln1_g=row(ln1_g), ln1_b=row(ln1_b), ln2_g=row(ln2_g), ln2_b=row(ln2_b),
        ffn_gate=_pad_ff(ffn_w_gate, 2), ffn_up=_pad_ff(ffn_w_up, 2), ffn_down=_pad_ff(ffn_w_down, 1),
        w_router=jnp.pad(w_router, ((0, 0), (0, 0), (0, ROUTER_PAD - N_EXPERTS))).astype(BF16),
        moe_gate=_pad_ff(moe_w_gate, 3), moe_up=_pad_ff(moe_w_up, 3), moe_down=_pad_ff(moe_w_down, 2),
        dft64=_dft64_table(),
    )


def _trunk(x3, p):
    B, S, _ = x3.shape
    x = x3.reshape(B * S, D_MODEL)
    rope_cos, rope_sin = _rope_tables(S)
    ct, snt = _dft_tables(S)
    for l in range(DEPTH):
        lambda_init = 0.8 - 0.6 * math.exp(-0.3 * l)
        g, so, qt, k, vat = _inproj(x, p["w_a"][l], p["dft64"], p["vn_g"][l], p["vn_b"][l],
                                    p["sgu_wcat"][l], p["sgu_bias"][l], rope_cos, rope_sin, S)
        fo = _fourier(g, ct, snt, B, S)
        do = _attn(p["lam"][l], qt, k, vat, p["subln_g"][l], B, S, lambda_init)
        j = l // 2
        moe = l % 2 == 1
        x1, route, counts = _merge(x, fo, so, do, p["w_g"][l], p["w_f"][l], p["w_s"][l], p["w_d"][l],
                                   p["w_o"][l], p["ln1_g"][l], p["ln1_b"][l],
                                   p["w_router"][j] if moe else None, S)
        if moe:
            tabs = _route_tables(route, counts, B * S)
            xs = _dispatch(x1, tabs, S)
            ys = _gmm(xs, tabs, p["moe_gate"][j], p["moe_up"][j], p["moe_down"][j], S)
            x = _combine(x1, route, tabs, ys, p["ln2_g"][l], p["ln2_b"][l], S)
        else:
            x = _ffn(x1, p["ffn_gate"][j], p["ffn_up"][j], p["ffn_down"][j],
                     p["ln2_g"][l], p["ln2_b"][l], S)
    return x.reshape(B, S, D_MODEL)


def kernel(x_prompt, x_sample, w_in, w_fourier, w_sgu, w_diff, w_out, vn_g, vn_b, sgu_w, sgu_b,
           lam_q1, lam_k1, lam_q2, lam_k2, subln_g, ln1_g, ln1_b, ln2_g, ln2_b,
           ffn_w_gate, ffn_w_up, ffn_w_down, w_router, moe_w_gate, moe_w_up, moe_w_down):
    p = _prepare(w_in, w_fourier, w_sgu, w_diff, w_out, vn_g, vn_b, sgu_w, sgu_b,
                 lam_q1, lam_k1, lam_q2, lam_k2, subln_g, ln1_g, ln1_b, ln2_g, ln2_b,
                 ffn_w_gate, ffn_w_up, ffn_w_down, w_router, moe_w_gate, moe_w_up, moe_w_down)
    return (_trunk(x_prompt, p), _trunk(x_sample, p))
```

```python
import functools
import math

import jax
import jax.numpy as jnp
from jax import lax
from jax.experimental import pallas as pl
from jax.experimental.pallas import tpu as pltpu

F32 = jnp.float32
BF16 = jnp.bfloat16

D_MODEL = 1024
DEPTH = 2
N_FOURIER_GROUPS = 4
FOURIER_GROUP = 64
FOURIER_WIDTH = N_FOURIER_GROUPS * FOURIER_GROUP
SGU_HEADS = 4
SGU_HEAD_DIM = 64
SGU_WIDTH = SGU_HEADS * SGU_HEAD_DIM
CHUNK = 128
DIFF_HEADS = 4
DIFF_HEAD_DIM = 64
DIFF_V_DIM = 2 * DIFF_HEAD_DIM
DIFF_QK_WIDTH = DIFF_HEADS * 2 * DIFF_HEAD_DIM
DIFF_V_WIDTH = DIFF_HEADS * DIFF_V_DIM
ROPE_THETA = 10000.0
N_BRANCHES = 3
D_FF = 2752
N_EXPERTS = 8
ALPHA = (2 * DEPTH) ** 0.25
LN_EPS = 1e-5
RMS_EPS = 1e-5

OFF_F = 0
OFF_U = OFF_F + FOURIER_WIDTH
OFF_V = OFF_U + SGU_WIDTH
OFF_Q = OFF_V + SGU_WIDTH
OFF_K = OFF_Q + DIFF_QK_WIDTH
OFF_VA = OFF_K + DIFF_QK_WIDTH
OFF_G = OFF_VA + DIFF_V_WIDTH
IN_WIDTH = OFF_G + N_BRANCHES * D_MODEL

TOP_K = 2
LANES = 128
SUBLANES = 8
MXU_WIDTH = 256
MOE_ROWS = 512
DFT_ROW_SPLIT = 64
SCORE_LEAD = 2
D_FF_PAD = 2816
ROUTER_PAD = LANES
MIB = 1024 * 1024


def _tiles(S):
    t = dict(
        tm_in=min(512, S),
        tm_comb=min(256, S),
        tm_dft=min(512, S),
        tq=min(512, S),
        tkc=min(128, S),
        tkp=min(512, S),
        tf=D_FF_PAD // 2,
    )
    return t


def _layer_norm(z, g, b):
    mu = jnp.mean(z, axis=-1, keepdims=True)
    zc = z - mu
    var = jnp.mean(zc * zc, axis=-1, keepdims=True)
    return zc * lax.rsqrt(var + LN_EPS) * g + b


def _dot(a, b):
    return jnp.dot(a, b, preferred_element_type=F32)


def _inproj_kernel(x_ref, w_ref, dft_ref, vng_ref, vnb_ref, sguw_ref, sgub_ref, cos_ref, sin_ref,
                   g_out, so_out, qt_out, k_out, vat_out):
    tm = x_ref.shape[0]
    xb = x_ref[...].astype(BF16)

    def proj(a, b):
        return _dot(xb, w_ref[:, a:b])

    f = proj(OFF_F, OFF_U)
    g_out[...] = _dot(f.astype(BF16), dft_ref[...]).astype(BF16)

    u = proj(OFF_U, OFF_V)
    v = proj(OFF_V, OFF_Q)
    vn = _layer_norm(v, vng_ref[...], vnb_ref[...]).astype(BF16)
    row_head = lax.broadcasted_iota(jnp.int32, (SGU_HEADS * CHUNK, SGU_WIDTH), 0) // CHUNK
    col_head = lax.broadcasted_iota(jnp.int32, (SGU_HEADS * CHUNK, SGU_WIDTH), 1) // SGU_HEAD_DIM
    head_mask = row_head == col_head
    for c in range(tm // CHUNK):
        rows = slice(c * CHUNK, (c + 1) * CHUNK)
        vb = vn[rows, :]
        rhs = jnp.where(head_mask, jnp.concatenate([vb] * SGU_HEADS, axis=0), jnp.zeros((), BF16))
        mixed = _dot(sguw_ref[...], rhs) + sgub_ref[...]
        so_out[rows, :] = (u[rows, :] * mixed).astype(BF16)

    cs = cos_ref[...]
    sn = sin_ref[...]
    lane = lax.broadcasted_iota(jnp.int32, (tm, LANES), 1)
    first_half = (lane % DIFF_HEAD_DIM) < (DIFF_HEAD_DIM // 2)

    def rope(t):
        outs = []
        for j in range(DIFF_QK_WIDTH // LANES):
            tc = t[:, j * LANES:(j + 1) * LANES]
            rot = jnp.where(first_half,
                            pltpu.roll(tc, LANES - DIFF_HEAD_DIM // 2, 1),
                            pltpu.roll(tc, DIFF_HEAD_DIM // 2, 1))
            outs.append(tc * cs + rot * sn)
        return jnp.concatenate(outs, axis=1)

    q = rope(proj(OFF_Q, OFF_K)) * (DIFF_HEAD_DIM ** -0.5)
    qt_out[...] = q.T.astype(BF16)
    k_out[...] = rope(proj(OFF_K, OFF_VA)).astype(BF16)
    vat_out[...] = proj(OFF_VA, OFF_G).T.astype(BF16)


def _inproj(x, w_a, dft64, vn_g, vn_b, sgu_wcat, sgu_bias, rope_cos, rope_sin, S):
    T = x.shape[0]
    tm = _tiles(S)["tm_in"]
    n_pos = S // tm
    full = lambda shape: pl.BlockSpec(shape, lambda i: (0,) * len(shape))
    return pl.pallas_call(
        _inproj_kernel,
        grid=(T // tm,),
        in_specs=[
            pl.BlockSpec((tm, D_MODEL), lambda i: (i, 0)),
            full((D_MODEL, OFF_G)),
            full((FOURIER_WIDTH, 2 * FOURIER_WIDTH)),
            full((1, SGU_WIDTH)),
            full((1, SGU_WIDTH)),
            full((CHUNK, SGU_HEADS * CHUNK)),
            full((CHUNK, SGU_WIDTH)),
            pl.BlockSpec((tm, LANES), lambda i: (i % n_pos, 0)),
            pl.BlockSpec((tm, LANES), lambda i: (i % n_pos, 0)),
        ],
        out_specs=[
            pl.BlockSpec((tm, 2 * FOURIER_WIDTH), lambda i: (i, 0)),
            pl.BlockSpec((tm, SGU_WIDTH), lambda i: (i, 0)),
            pl.BlockSpec((DIFF_QK_WIDTH, tm), lambda i: (0, i)),
            pl.BlockSpec((tm, DIFF_QK_WIDTH), lambda i: (i, 0)),
            pl.BlockSpec((DIFF_V_WIDTH, tm), lambda i: (0, i)),
        ],
        out_shape=[
            jax.ShapeDtypeStruct((T, 2 * FOURIER_WIDTH), BF16),
            jax.ShapeDtypeStruct((T, SGU_WIDTH), BF16),
            jax.ShapeDtypeStruct((DIFF_QK_WIDTH, T), BF16),
            jax.ShapeDtypeStruct((T, DIFF_QK_WIDTH), BF16),
            jax.ShapeDtypeStruct((DIFF_V_WIDTH, T), BF16),
        ],
        compiler_params=pltpu.CompilerParams(
            dimension_semantics=("parallel",), vmem_limit_bytes=48 * MIB),
        name="inproj",
    )(x, w_a, dft64, vn_g, vn_b, sgu_wcat, sgu_bias, rope_cos, rope_sin)


def _fourier_kernel(ct_ref, snt_ref, g_ref, o_ref):
    acc = _dot(ct_ref[...], g_ref[:, :FOURIER_WIDTH])
    acc = acc + _dot(snt_ref[...], g_ref[:, FOURIER_WIDTH:])
    o_ref[...] = acc.astype(BF16)


def _fourier(g, ct, snt, B, S):
    tm = _tiles(S)["tm_dft"]
    n_row = S // tm
    return pl.pallas_call(
        _fourier_kernel,
        grid=(n_row, B),
        in_specs=[
            pl.BlockSpec((tm, S), lambda i, b: (i, 0), pipeline_mode=pl.Buffered(1)),
            pl.BlockSpec((tm, S), lambda i, b: (i, 0), pipeline_mode=pl.Buffered(1)),
            pl.BlockSpec((S, 2 * FOURIER_WIDTH), lambda i, b: (b, 0)),
        ],
        out_specs=pl.BlockSpec((tm, FOURIER_WIDTH), lambda i, b: (b * n_row + i, 0)),
        out_shape=jax.ShapeDtypeStruct((B * S, FOURIER_WIDTH), BF16),
        compiler_params=pltpu.CompilerParams(
            dimension_semantics=("parallel", "parallel"), vmem_limit_bytes=48 * MIB),
        name="fourier",
    )(ct, snt, g)


def _attn_kernel(lam_ref, qt_ref, k_ref, vt_ref, g_ref, o_ref, s_ref, e_ref, m_ref, *, tkp, lambda_init):
    tq = qt_ref.shape[1]
    width = 2 * tq
    S = k_ref.shape[0]
    n_p = S // tkp
    groups = tkp // SUBLANES
    ring = s_ref.shape[0]
    lp = lam_ref[...]
    lam = (jnp.exp(jnp.sum(lp[0:1, :] * lp[1:2, :], axis=-1, keepdims=True))
           - jnp.exp(jnp.sum(lp[2:3, :] * lp[3:4, :], axis=-1, keepdims=True))
           + lambda_init)
    qt = qt_ref[...]
    row = lax.broadcasted_iota(jnp.int32, qt.shape, 0)
    zero = jnp.zeros((), BF16)
    qt_both = jnp.concatenate(
        [jnp.where(row < DIFF_HEAD_DIM, qt, zero), jnp.where(row >= DIFF_HEAD_DIM, qt, zero)], axis=1)

    m_class = jnp.full((SUBLANES, width), -jnp.inf, F32)
    m_exp = m_class
    l_class = jnp.zeros((SUBLANES, width), F32)
    l_hist = []
    for p in range(n_p + 1):
        if p < n_p:
            hold = 0
            if p >= ring:
                bits = pltpu.bitcast(l_hist[p - ring], jnp.int32)[0, 0]
                half = jnp.int32(16)
                hold = lax.shift_right_logical(lax.shift_right_logical(bits, half), half)
            kc = k_ref[pl.ds(pl.multiple_of(p * tkp + hold, tkp), tkp), :]
            s = _dot(kc, qt_both)
            s_ref[p % ring] = s
            m_class = jnp.maximum(m_class, jnp.max(s.reshape(groups, SUBLANES, width), axis=0))
            m_ref[p] = m_class
        if p >= 1:
            j = p - 1
            m_new = m_ref[j]
            e = jnp.exp(s_ref[j % ring].reshape(groups, SUBLANES, width) - m_new[None])
            e_ref[j * tkp:(j + 1) * tkp, :] = e.reshape(tkp, width).astype(BF16)
            l_class = l_class * jnp.exp(m_exp - m_new) + jnp.sum(e, axis=0)
            m_exp = m_new
            l_hist.append(l_class)

    m_all = jnp.max(m_exp, axis=0, keepdims=True)
    l_all = jnp.sum(l_class * jnp.exp(m_exp - m_all), axis=0, keepdims=True)
    scale = jnp.concatenate([1.0 / l_all[:, :tq], lam / l_all[:, tq:]], axis=1)
    packed = 2 * SUBLANES
    acc = jnp.zeros((DIFF_V_DIM, tq), F32)
    for p in range(n_p):
        f = jnp.exp(m_ref[p] - m_all) * scale
        f = jnp.concatenate([f, f], axis=0).astype(BF16)
        ew = e_ref[p * tkp:(p + 1) * tkp, :].reshape(tkp // packed, packed, width) * f[None]
        a = (ew[:, :, :tq] - ew[:, :, tq:]).reshape(tkp, tq)
        acc = acc + _dot(vt_ref[:, p * tkp:(p + 1) * tkp], a)

    o = acc.T
    o = o * lax.rsqrt(jnp.mean(o * o, axis=-1, keepdims=True) + RMS_EPS) * g_ref[...]
    o_ref[...] = (o * (1.0 - lambda_init)).astype(BF16)


def _attn(lam_params, qt, k, vat, subln_g, B, S, lambda_init):
    t = _tiles(S)
    tq, tkp = t["tq"], t["tkp"]
    n_q = S // tq
    kern = functools.partial(_attn_kernel, tkp=tkp, lambda_init=lambda_init)
    return pl.pallas_call(
        kern,
        grid=(B, DIFF_HEADS, n_q),
        in_specs=[
            pl.BlockSpec((4, DIFF_HEAD_DIM), lambda b, h, i: (0, 0)),
            pl.BlockSpec((DIFF_V_DIM, tq), lambda b, h, i: (h, b * n_q + i)),
            pl.BlockSpec((S, DIFF_V_DIM), lambda b, h, i: (b, h)),
            pl.BlockSpec((DIFF_V_DIM, S), lambda b, h, i: (h, b)),
            pl.BlockSpec((1, DIFF_V_DIM), lambda b, h, i: (0, 0)),
        ],
        out_specs=pl.BlockSpec((tq, DIFF_V_DIM), lambda b, h, i: (b * n_q + i, h)),
        out_shape=jax.ShapeDtypeStruct((B * S, DIFF_V_WIDTH), BF16),
        scratch_shapes=[pltpu.VMEM((min(SCORE_LEAD, S // tkp), tkp, 2 * tq), F32),
                        pltpu.VMEM((S, 2 * tq), BF16),
                        pltpu.VMEM((S // tkp, SUBLANES, 2 * tq), F32)],
        compiler_params=pltpu.CompilerParams(
            dimension_semantics=("parallel", "parallel", "parallel"), vmem_limit_bytes=48 * MIB),
        name="attn",
    )(lam_params, qt, k, vat, subln_g)


def _merge_kernel(*refs, with_router):
    if with_router:
        (x_ref, fo_ref, so_ref, do_ref, wg_ref, wf_ref, ws_ref, wd_ref, wo_ref, lg_ref, lb_ref,
         wr_ref, x1_out, route_out, counts_out, base_ref) = refs
    else:
        (x_ref, fo_ref, so_ref, do_ref, wg_ref, wf_ref, ws_ref, wd_ref, wo_ref, lg_ref, lb_ref,
         x1_out) = refs
    x = x_ref[...]
    xb = x.astype(BF16)
    merged = None
    for j, (br_ref, wb_ref) in enumerate(((fo_ref, wf_ref), (so_ref, ws_ref), (do_ref, wd_ref))):
        gate = jax.nn.sigmoid(_dot(xb, wg_ref[:, j * D_MODEL:(j + 1) * D_MODEL]))
        term = gate * _dot(br_ref[...], wb_ref[...])
        merged = term if merged is None else merged + term
    y = _dot(merged.astype(BF16), wo_ref[...])
    x1 = _layer_norm(ALPHA * x + y, lg_ref[...], lb_ref[...])
    x1_out[...] = x1

    if with_router:
        tm = x.shape[0]
        logits = _dot(x1.astype(BF16), wr_ref[...])
        lane = lax.broadcasted_iota(jnp.int32, (tm, ROUTER_PAD), 1)
        neg = jnp.float32(-jnp.inf)
        lg = jnp.where(lane < N_EXPERTS, logits, neg)
        v1 = jnp.max(lg, axis=-1, keepdims=True)
        i1 = jnp.min(jnp.where(lg == v1, lane, ROUTER_PAD), axis=-1, keepdims=True)
        lg2 = jnp.where(lane == i1, neg, lg)
        v2 = jnp.max(lg2, axis=-1, keepdims=True)
        i2 = jnp.min(jnp.where(lg2 == v2, lane, ROUTER_PAD), axis=-1, keepdims=True)
        e2 = jnp.exp(v2 - v1)
        den = 1.0 + e2

        @pl.when(pl.program_id(0) == 0)
        def _():
            base_ref[...] = jnp.zeros_like(base_ref)

        pick1 = lane == i1
        pick2 = lane == i2
        sel = jnp.logical_or(pick1, pick2)
        r_id = lax.broadcasted_iota(jnp.int32, (tm, tm), 0)
        c_id = lax.broadcasted_iota(jnp.int32, (tm, tm), 1)
        ltri = (c_id < r_id).astype(BF16)
        base = base_ref[0:1, :]
        rank = base + _dot(ltri, sel.astype(BF16))
        r1 = jnp.sum(jnp.where(pick1, rank, 0.0), axis=-1, keepdims=True)
        r2 = jnp.sum(jnp.where(pick2, rank, 0.0), axis=-1, keepdims=True)
        new_base = base + jnp.sum(sel.astype(F32), axis=0, keepdims=True)
        base_ref[...] = jnp.broadcast_to(new_base, base_ref.shape)
        counts_out[...] = jnp.broadcast_to(new_base, counts_out.shape)
        fields = (1.0 / den, e2 / den, i1.astype(F32), i2.astype(F32), r1, r2)
        route = jnp.zeros((tm, ROUTER_PAD), F32)
        for n, val in enumerate(fields):
            route = jnp.where(lane == n, val, route)
        route_out[...] = route


def _merge(x, fo, so, do, w_g, w_f, w_s, w_d, w_o, ln_g, ln_b, w_router, S):
    T = x.shape[0]
    tm = _tiles(S)["tm_in"]
    with_router = w_router is not None
    full = lambda shape: pl.BlockSpec(shape, lambda i: (0,) * len(shape))
    row = lambda width: pl.BlockSpec((tm, width), lambda i: (i, 0))
    in_specs = [
        row(D_MODEL), row(FOURIER_WIDTH), row(SGU_WIDTH), row(DIFF_V_WIDTH),
        full((D_MODEL, N_BRANCHES * D_MODEL)),
        full((FOURIER_WIDTH, D_MODEL)), full((SGU_WIDTH, D_MODEL)), full((DIFF_V_WIDTH, D_MODEL)),
        full((D_MODEL, D_MODEL)), full((1, D_MODEL)), full((1, D_MODEL)),
    ]
    args = [x, fo, so, do, w_g, w_f, w_s, w_d, w_o, ln_g, ln_b]
    out_specs = [row(D_MODEL)]
    out_shape = [jax.ShapeDtypeStruct((T, D_MODEL), F32)]
    scratch = []
    if with_router:
        in_specs.append(full((D_MODEL, ROUTER_PAD)))
        args.append(w_router)
        out_specs += [row(ROUTER_PAD), full((SUBLANES, ROUTER_PAD))]
        out_shape += [jax.ShapeDtypeStruct((T, ROUTER_PAD), F32),
                      jax.ShapeDtypeStruct((SUBLANES, ROUTER_PAD), F32)]
        scratch.append(pltpu.VMEM((SUBLANES, ROUTER_PAD), F32))
    outs = pl.pallas_call(
        functools.partial(_merge_kernel, with_router=with_router),
        grid=(T // tm,),
        in_specs=in_specs,
        out_specs=out_specs,
        out_shape=out_shape,
        scratch_shapes=scratch,
        compiler_params=pltpu.CompilerParams(
            dimension_semantics=("arbitrary" if with_router else "parallel",),
            vmem_limit_bytes=56 * MIB),
        name="merge_router" if with_router else "merge",
    )(*args)
    return (outs[0], outs[1], outs[2]) if with_router else (outs[0], None, None)


def _swiglu(xb, wg_ref, wu_ref, wd_ref, tf):
    y = None
    for f in range(D_FF_PAD // tf):
        cols = slice(f * tf, (f + 1) * tf)
        h = jax.nn.silu(_dot(xb, wg_ref[:, cols])) * _dot(xb, wu_ref[:, cols])
        part = _dot(h.astype(BF16), wd_ref[cols, :])
        y = part if y is None else y + part
    return y


def _ffn_kernel(x_ref, wg_ref, wu_ref, wd_ref, lg_ref, lb_ref, o_ref, *, tf):
    x = x_ref[...]
    ff = _swiglu(x.astype(BF16), wg_ref, wu_ref, wd_ref, tf)
    o_ref[...] = _layer_norm(ALPHA * x + ff, lg_ref[...], lb_ref[...])


def _ffn(x, w_gate, w_up, w_down, ln_g, ln_b, S):
    T = x.shape[0]
    t = _tiles(S)
    tm, tf = t["tm_in"], t["tf"]
    resident = lambda shape: pl.BlockSpec(shape, lambda i: (0, 0), pipeline_mode=pl.Buffered(1))
    return pl.pallas_call(
        functools.partial(_ffn_kernel, tf=tf),
        grid=(T // tm,),
        in_specs=[
            pl.BlockSpec((tm, D_MODEL), lambda i: (i, 0)),
            resident((D_MODEL, D_FF_PAD)), resident((D_MODEL, D_FF_PAD)), resident((D_FF_PAD, D_MODEL)),
            pl.BlockSpec((1, D_MODEL), lambda i: (0, 0)),
            pl.BlockSpec((1, D_MODEL), lambda i: (0, 0)),
        ],
        out_specs=pl.BlockSpec((tm, D_MODEL), lambda i: (i, 0)),
        out_shape=jax.ShapeDtypeStruct((T, D_MODEL), F32),
        compiler_params=pltpu.CompilerParams(
            dimension_semantics=("parallel",), vmem_limit_bytes=56 * MIB),
        name="ffn_dense",
    )(x, w_gate, w_up, w_down, ln_g, ln_b)


def _route_tables(route, counts, T):
    R = MOE_ROWS
    as_int = lambda n: route[:, n].astype(jnp.int32)
    i1, i2, r1, r2 = as_int(2), as_int(3), as_int(4), as_int(5)
    cnt = counts[0, :N_EXPERTS].astype(jnp.int32)
    padded = (cnt + R - 1) // R * R
    ends = jnp.cumsum(padded)
    off = ends - padded
    n_tiles = TOP_K * T // R + N_EXPERTS
    tile_expert = jnp.minimum(
        jnp.searchsorted(ends, jnp.arange(n_tiles, dtype=jnp.int32) * R, side="right"),
        N_EXPERTS - 1).astype(jnp.int32)
    pad_start = off + cnt
    pad_n = (padded - cnt).at[N_EXPERTS - 1].set(n_tiles * R - pad_start[N_EXPERTS - 1])
    return dict(
        pos1=jnp.take(off, i1) + r1, pos2=jnp.take(off, i2) + r2,
        tile_expert=tile_expert, n_used=(ends[-1:] // R).astype(jnp.int32),
        pad=jnp.concatenate([pad_start, pad_n]).astype(jnp.int32), n_tiles=n_tiles)


def _dispatch_kernel(pad_ref, x_ref, p1_ref, p2_ref, xs_ref, zero_ref, sem):
    tm = x_ref.shape[0]

    def row_copy(src_ref, src_row, dst_row):
        return pltpu.make_async_copy(src_ref.at[pl.ds(src_row, 1)], xs_ref.at[pl.ds(dst_row, 1)], sem)

    @pl.when(pl.program_id(0) == 0)
    def _():
        zero_ref[...] = jnp.zeros_like(zero_ref)
        for e in range(N_EXPERTS):
            start = pad_ref[e]
            n = pad_ref[N_EXPERTS + e]

            def issue_pad(k, c):
                row_copy(zero_ref, 0, start + k).start()
                return c

            def drain_pad(k, c):
                row_copy(zero_ref, 0, start + k).wait()
                return c

            lax.fori_loop(0, n, issue_pad, 0)
            lax.fori_loop(0, n, drain_pad, 0)

    def issue(t, c):
        row_copy(x_ref, t, p1_ref[0, 0, t]).start(priority=0)
        row_copy(x_ref, t, p2_ref[0, 0, t]).start(priority=1)
        return c

    def drain(t, c):
        row_copy(x_ref, t, p1_ref[0, 0, t]).wait()
        row_copy(x_ref, t, p2_ref[0, 0, t]).wait()
        return c

    lax.fori_loop(0, tm, issue, 0, unroll=8)
    lax.fori_loop(0, tm, drain, 0, unroll=8)


def _dispatch(x, tabs, S):
    T = x.shape[0]
    tm = _tiles(S)["tm_in"]
    n_rows = tabs["n_tiles"] * MOE_ROWS
    idx_spec = pl.BlockSpec((1, 1, tm), lambda i, pad: (i, 0, 0), memory_space=pltpu.SMEM)
    return pl.pallas_call(
        _dispatch_kernel,
        grid_spec=pltpu.PrefetchScalarGridSpec(
            num_scalar_prefetch=1,
            grid=(T // tm,),
            in_specs=[pl.BlockSpec((tm, D_MODEL), lambda i, pad: (i, 0)), idx_spec, idx_spec],
            out_specs=pl.BlockSpec(memory_space=pl.ANY),
            scratch_shapes=[pltpu.VMEM((SUBLANES, D_MODEL), F32), pltpu.SemaphoreType.DMA(())],
        ),
        out_shape=jax.ShapeDtypeStruct((n_rows, D_MODEL), F32),
        compiler_params=pltpu.CompilerParams(dimension_semantics=("arbitrary",)),
        name="moe_dispatch",
    )(tabs["pad"], x, tabs["pos1"].reshape(T // tm, 1, tm), tabs["pos2"].reshape(T // tm, 1, tm))


def _gmm_kernel(te_ref, nu_ref, x_ref, wg_ref, wu_ref, wd_ref, o_ref, *, tf):
    used = pl.program_id(0) < nu_ref[0]

    @pl.when(jnp.logical_not(used))
    def _():
        o_ref[...] = jnp.zeros_like(o_ref)

    @pl.when(used)
    def _():
        o_ref[...] = _swiglu(x_ref[...].astype(BF16), wg_ref.at[0], wu_ref.at[0], wd_ref.at[0], tf)


def _gmm(xs, tabs, w_gate, w_up, w_down, S):
    R = MOE_ROWS
    tf = _tiles(S)["tf"]
    row_spec = pl.BlockSpec((R, D_MODEL), lambda j, te, nu: (j, 0))
    w_spec = lambda shape: pl.BlockSpec(shape, lambda j, te, nu: (te[j], 0, 0),
                                        pipeline_mode=pl.Buffered(1))
    return pl.pallas_call(
        functools.partial(_gmm_kernel, tf=tf),
        grid_spec=pltpu.PrefetchScalarGridSpec(
            num_scalar_prefetch=2,
            grid=(tabs["n_tiles"],),
            in_specs=[row_spec, w_spec((1, D_MODEL, D_FF_PAD)), w_spec((1, D_MODEL, D_FF_PAD)),
                      w_spec((1, D_FF_PAD, D_MODEL))],
            out_specs=row_spec,
        ),
        out_shape=jax.ShapeDtypeStruct(xs.shape, F32),
        compiler_params=pltpu.CompilerParams(
            dimension_semantics=("arbitrary",), vmem_limit_bytes=56 * MIB),
        name="moe_gmm",
    )(tabs["tile_expert"], tabs["n_used"], xs, w_gate, w_up, w_down)


def _combine_kernel(x_ref, route_ref, p1_ref, p2_ref, lg_ref, lb_ref, ys_ref, o_ref, y1_ref, y2_ref, sem):
    tm = x_ref.shape[0]

    def row_copies(t):
        return (pltpu.make_async_copy(ys_ref.at[pl.ds(p1_ref[0, 0, t], 1)], y1_ref.at[pl.ds(t, 1)], sem),
                pltpu.make_async_copy(ys_ref.at[pl.ds(p2_ref[0, 0, t], 1)], y2_ref.at[pl.ds(t, 1)], sem))

    def issue(t, c):
        for priority, cp in enumerate(row_copies(t)):
            cp.start(priority=priority)
        return c

    def drain(t, c):
        for cp in row_copies(t):
            cp.wait()
        return c

    lax.fori_loop(0, tm, issue, 0, unroll=8)
    lax.fori_loop(0, tm, drain, 0, unroll=8)
    route = route_ref[...]
    ff = route[:, 0:1] * y1_ref[...] + route[:, 1:2] * y2_ref[...]
    o_ref[...] = _layer_norm(ALPHA * x_ref[...] + ff, lg_ref[...], lb_ref[...])


def _combine(x, route, tabs, ys, ln_g, ln_b, S):
    T = x.shape[0]
    tm = _tiles(S)["tm_comb"]
    idx_spec = pl.BlockSpec((1, 1, tm), lambda i: (i, 0, 0), memory_space=pltpu.SMEM)
    return pl.pallas_call(
        _combine_kernel,
        grid=(T // tm,),
        in_specs=[
            pl.BlockSpec((tm, D_MODEL), lambda i: (i, 0)),
            pl.BlockSpec((tm, ROUTER_PAD), lambda i: (i, 0)),
            idx_spec, idx_spec,
            pl.BlockSpec((1, D_MODEL), lambda i: (0, 0)),
            pl.BlockSpec((1, D_MODEL), lambda i: (0, 0)),
            pl.BlockSpec(memory_space=pl.ANY),
        ],
        out_specs=pl.BlockSpec((tm, D_MODEL), lambda i: (i, 0)),
        out_shape=jax.ShapeDtypeStruct((T, D_MODEL), F32),
        scratch_shapes=[pltpu.VMEM((tm, D_MODEL), F32), pltpu.VMEM((tm, D_MODEL), F32),
                        pltpu.SemaphoreType.DMA(())],
        compiler_params=pltpu.CompilerParams(dimension_semantics=("arbitrary",)),
        name="moe_combine",
    )(x, route, tabs["pos1"].reshape(T // tm, 1, tm), tabs["pos2"].reshape(T // tm, 1, tm), ln_g, ln_b, ys)


def _rope_tables(S):
    half = DIFF_HEAD_DIM // 2
    inv = ROPE_THETA ** (-jnp.arange(0, DIFF_HEAD_DIM, 2, dtype=F32) / DIFF_HEAD_DIM)
    ang = jnp.arange(S, dtype=F32)[:, None] * inv[None, :]
    cos, sin = jnp.cos(ang), jnp.sin(ang)
    reps = LANES // DIFF_HEAD_DIM
    cos_t = jnp.tile(jnp.concatenate([cos, cos], axis=1), (1, reps))
    sin_t = jnp.tile(jnp.concatenate([-sin, sin], axis=1), (1, reps))
    assert cos_t.shape == (S, LANES) and half * 2 == DIFF_HEAD_DIM
    return cos_t, sin_t


def _dft_tables(S):
    split = math.gcd(S, DFT_ROW_SPLIT)
    k = jnp.arange(S, dtype=jnp.int32)[None, :]

    def thin(rows):
        ang = ((rows[:, None] * k) % S).astype(F32) * (2.0 * math.pi / S)
        return jnp.cos(ang), jnp.sin(ang)

    ch, sh = thin(jnp.arange(S // split, dtype=jnp.int32) * split)
    cl, sl = thin(jnp.arange(split, dtype=jnp.int32))
    scale = S ** -0.5
    cos = (ch[:, None, :] * cl[None, :, :] - sh[:, None, :] * sl[None, :, :]) * scale
    sin = (sh[:, None, :] * cl[None, :, :] + ch[:, None, :] * sl[None, :, :]) * -scale
    return cos.reshape(S, S).astype(BF16), sin.reshape(S, S).astype(BF16)


def _dft64_table():
    j = jnp.arange(FOURIER_GROUP, dtype=jnp.int32)
    ang = ((j[:, None] * j[None, :]) % FOURIER_GROUP).astype(F32) * (2.0 * math.pi / FOURIER_GROUP)
    scale = FOURIER_GROUP ** -0.5
    eye = jnp.eye(N_FOURIER_GROUPS, dtype=F32)
    return jnp.concatenate([jnp.kron(eye, jnp.cos(ang) * scale),
                            jnp.kron(eye, jnp.sin(ang) * scale)], axis=1).astype(BF16)


def _pad_ff(w, axis):
    pad = [(0, 0)] * w.ndim
    pad[axis] = (0, D_FF_PAD - D_FF)
    return jnp.pad(w, pad).astype(BF16)


def _prepare(w_in,w_fourier, w_sgu, w_diff, w_out, vn_g, vn_b, sgu_w, sgu_b,
             lam_q1, lam_k1, lam_q2, lam_k2, subln_g, ln1_g, ln1_b, ln2_g, ln2_b,
             ffn_w_gate, ffn_w_up, ffn_w_down, w_router, moe_w_gate, moe_w_up, moe_w_down):
    w_in_b = w_in.astype(BF16)
    row = lambda a: a.astype(F32)[:, None, :]
    return dict(
        w_a=w_in_b[:, :, :OFF_G], w_g=w_in_b[:, :, OFF_G:],
        w_f=w_fourier.astype(BF16), w_s=w_sgu.astype(BF16), w_d=w_diff.astype(BF16),
        w_o=w_out.astype(BF16),
        vn_g=row(vn_g), vn_b=row(vn_b),
        sgu_wcat=jnp.transpose(sgu_w, (0, 2, 1, 3)).reshape(DEPTH, CHUNK, SGU_HEADS * CHUNK).astype(BF16),
        sgu_bias=jnp.repeat(jnp.transpose(sgu_b, (0, 2, 1)).astype(F32), SGU_HEAD_DIM, axis=2),
        lam=jnp.stack([lam_q1, lam_k1, lam_q2, lam_k2], axis=1).astype(F32),
        subln_g=row(subln_g), ln1_g=row(ln1_g), ln1_b=row(ln1_b), ln2_g=row(ln2_g), ln2_b=row(ln2_b),
        ffn_gate=_pad_ff(ffn_w_gate, 2), ffn_up=_pad_ff(ffn_w_up, 2), ffn_down=_pad_ff(ffn_w_down, 1),
        w_router=jnp.pad(w_router, ((0, 0), (0, 0), (0, ROUTER_PAD - N_EXPERTS))).astype(BF16),
        moe_gate=_pad_ff(moe_w_gate, 3), moe_up=_pad_ff(moe_w_up, 3), moe_down=_pad_ff(moe_w_down, 2),
        dft64=_dft64_table(),
    )


def _trunk(x3, p):
    B, S, _ = x3.shape
    x = x3.reshape(B * S, D_MODEL)
    rope_cos, rope_sin = _rope_tables(S)
    ct, snt = _dft_tables(S)
    for l in range(DEPTH):
        lambda_init = 0.8 - 0.6 * math.exp(-0.3 * l)
        g, so, qt, k, vat = _inproj(x, p["w_a"][l], p["dft64"], p["vn_g"][l], p["vn_b"][l],
                                    p["sgu_wcat"][l], p["sgu_bias"][l], rope_cos, rope_sin, S)
        fo = _fourier(g, ct, snt, B, S)
        do = _attn(p["lam"][l], qt, k, vat, p["subln_g"][l], B, S, lambda_init)
        j = l // 2
        moe = l % 2 == 1
        x1, route, counts = _merge(x, fo, so, do, p["w_g"][l], p["w_f"][l], p["w_s"][l], p["w_d"][l],
                                   p["w_o"][l], p["ln1_g"][l], p["ln1_b"][l],
                                   p["w_router"][j] if moe else None, S)
        if moe:
            tabs = _route_tables(route, counts, B * S)
            xs = _dispatch(x1, tabs, S)
            ys = _gmm(xs, tabs, p["moe_gate"][j], p["moe_up"][j], p["moe_down"][j], S)
            x = _combine(x1, route, tabs, ys, p["ln2_g"][l], p["ln2_b"][l], S)
        else:
            x = _ffn(x1, p["ffn_gate"][j], p["ffn_up"][j], p["ffn_down"][j],
                     p["ln2_g"][l], p["ln2_b"][l], S)
    return x.reshape(B, S, D_MODEL)


def kernel(x_prompt, x_sample, w_in, w_fourier, w_sgu, w_diff, w_out, vn_g, vn_b, sgu_w, sgu_b,
           lam_q1, lam_k1, lam_q2, lam_k2, subln_g, ln1_g, ln1_b, ln2_g, ln2_b,
           ffn_w_gate, ffn_w_up, ffn_w_down, w_router, moe_w_gate, moe_w_up, moe_w_down):
    p = _prepare(w_in, w_fourier, w_sgu, w_diff, w_out, vn_g, vn_b, sgu_w, sgu_b,
                 lam_q1, lam_k1, lam_q2, lam_k2, subln_g, ln1_g, ln1_b, ln2_g, ln2_b,
                 ffn_w_gate, ffn_w_up, ffn_w_down, w_router, moe_w_gate, moe_w_up, moe_w_down)
    return (_trunk(x_prompt, p), _trunk(x_sample, p))
```

```python
import functools
import math

import jax
import jax.numpy as jnp
from jax import lax
from jax.experimental import pallas as pl
from jax.experimental.pallas import tpu as pltpu

F32 = jnp.float32
BF16 = jnp.bfloat16

D_MODEL = 1024
DEPTH = 2
N_FOURIER_GROUPS = 4
FOURIER_GROUP = 64
FOURIER_WIDTH = N_FOURIER_GROUPS * FOURIER_GROUP
SGU_HEADS = 4
SGU_HEAD_DIM = 64
SGU_WIDTH = SGU_HEADS * SGU_HEAD_DIM
CHUNK = 128
DIFF_HEADS = 4
DIFF_HEAD_DIM = 64
DIFF_V_DIM = 2 * DIFF_HEAD_DIM
DIFF_QK_WIDTH = DIFF_HEADS * 2 * DIFF_HEAD_DIM
DIFF_V_WIDTH = DIFF_HEADS * DIFF_V_DIM
ROPE_THETA = 10000.0
N_BRANCHES = 3
D_FF = 2752
N_EXPERTS = 8
ALPHA = (2 * DEPTH) ** 0.25
LN_EPS = 1e-5
RMS_EPS = 1e-5

OFF_F = 0
OFF_U = OFF_F + FOURIER_WIDTH
OFF_V = OFF_U + SGU_WIDTH
OFF_Q = OFF_V + SGU_WIDTH
OFF_K = OFF_Q + DIFF_QK_WIDTH
OFF_VA = OFF_K + DIFF_QK_WIDTH
OFF_G = OFF_VA + DIFF_V_WIDTH
IN_WIDTH = OFF_G + N_BRANCHES * D_MODEL

TOP_K = 2
LANES = 128
SUBLANES = 8
MOE_ROWS = 512
DFT_ROW_SPLIT = 64
SCORE_LEAD = 4
D_FF_PAD = 2816
ROUTER_PAD = LANES
MIB = 1024 * 1024


def _tiles(S):
    t = dict(
        tm_in=min(512, S),
        tm_comb=min(256, S),
        tm_dft=min(512, S),
        tq=min(512, S),
        tkc=min(128, S),
        tkp=min(512, S),
        tf=D_FF_PAD // 2,
    )
    return t


def _layer_norm(z, g, b):
    mu = jnp.mean(z, axis=-1, keepdims=True)
    zc = z - mu
    var = jnp.mean(zc * zc, axis=-1, keepdims=True)
    return zc * lax.rsqrt(var + LN_EPS) * g + b


def _dot(a, b):
    return jnp.dot(a, b, preferred_element_type=F32)


def _inproj_kernel(x_ref, w_ref, dft_ref, vng_ref, vnb_ref, sguw_ref, sgub_ref, cos_ref, sin_ref,
                   g_out, so_out, qt_out, k_out, vat_out):
    tm = x_ref.shape[0]
    xb = x_ref[...].astype(BF16)

    def proj(a, b):
        return _dot(xb, w_ref[:, a:b])

    f = proj(OFF_F, OFF_U)
    g_out[...] = _dot(f.astype(BF16), dft_ref[...]).astype(BF16)

    u = proj(OFF_U, OFF_V)
    v = proj(OFF_V, OFF_Q)
    vn = _layer_norm(v, vng_ref[...], vnb_ref[...]).astype(BF16)
    row_head = lax.broadcasted_iota(jnp.int32, (SGU_HEADS * CHUNK, SGU_WIDTH), 0) // CHUNK
    col_head = lax.broadcasted_iota(jnp.int32, (SGU_HEADS * CHUNK, SGU_WIDTH), 1) // SGU_HEAD_DIM
    head_mask = row_head == col_head
    for c in range(tm // CHUNK):
        rows = slice(c * CHUNK, (c + 1) * CHUNK)
        vb = vn[rows, :]
        rhs = jnp.where(head_mask, jnp.concatenate([vb] * SGU_HEADS, axis=0), jnp.zeros((), BF16))
        mixed = _dot(sguw_ref[...], rhs) + sgub_ref[...]
        so_out[rows, :] = (u[rows, :] * mixed).astype(BF16)

    cs = cos_ref[...]
    sn = sin_ref[...]
    lane = lax.broadcasted_iota(jnp.int32, (tm, LANES), 1)
    first_half = (lane % DIFF_HEAD_DIM) < (DIFF_HEAD_DIM // 2)

    def rope(t):
        outs = []
        for j in range(DIFF_QK_WIDTH // LANES):
            tc = t[:, j * LANES:(j + 1) * LANES]
            rot = jnp.where(first_half,
                            pltpu.roll(tc, LANES - DIFF_HEAD_DIM // 2, 1),
                            pltpu.roll(tc, DIFF_HEAD_DIM // 2, 1))
            outs.append(tc * cs + rot * sn)
        return jnp.concatenate(outs, axis=1)

    q = rope(proj(OFF_Q, OFF_K)) * (DIFF_HEAD_DIM ** -0.5 * math.log2(math.e))
    qt_out[...] = q.T.astype(BF16)
    k_out[...] = rope(proj(OFF_K, OFF_VA)).astype(BF16)
    vat_out[...] = proj(OFF_VA, OFF_G).T.astype(BF16)


def _inproj(x, w_a, dft64, vn_g, vn_b, sgu_wcat, sgu_bias, rope_cos, rope_sin, S):
    T = x.shape[0]
    tm = _tiles(S)["tm_in"]
    n_pos = S // tm
    full = lambda shape: pl.BlockSpec(shape, lambda i: (0,) * len(shape))
    return pl.pallas_call(
        _inproj_kernel,
        grid=(T // tm,),
        in_specs=[
            pl.BlockSpec((tm, D_MODEL), lambda i: (i, 0)),
            full((D_MODEL, OFF_G)),
            full((FOURIER_WIDTH, 2 * FOURIER_WIDTH)),
            full((1, SGU_WIDTH)),
            full((1, SGU_WIDTH)),
            full((CHUNK, SGU_HEADS * CHUNK)),
            full((CHUNK, SGU_WIDTH)),
            pl.BlockSpec((tm, LANES), lambda i: (i % n_pos, 0)),
            pl.BlockSpec((tm, LANES), lambda i: (i % n_pos, 0)),
        ],
        out_specs=[
            pl.BlockSpec((tm, 2 * FOURIER_WIDTH), lambda i: (i, 0)),
            pl.BlockSpec((tm, SGU_WIDTH), lambda i: (i, 0)),
            pl.BlockSpec((DIFF_QK_WIDTH, tm), lambda i: (0, i)),
            pl.BlockSpec((tm, DIFF_QK_WIDTH), lambda i: (i, 0)),
            pl.BlockSpec((DIFF_V_WIDTH, tm), lambda i: (0, i)),
        ],
        out_shape=[
            jax.ShapeDtypeStruct((T, 2 * FOURIER_WIDTH), BF16),
            jax.ShapeDtypeStruct((T, SGU_WIDTH), BF16),
            jax.ShapeDtypeStruct((DIFF_QK_WIDTH, T), BF16),
            jax.ShapeDtypeStruct((T, DIFF_QK_WIDTH), BF16),
            jax.ShapeDtypeStruct((DIFF_V_WIDTH, T), BF16),
        ],
        compiler_params=pltpu.CompilerParams(
            dimension_semantics=("parallel",), vmem_limit_bytes=48 * MIB),
        name="inproj",
    )(x, w_a, dft64, vn_g, vn_b, sgu_wcat, sgu_bias, rope_cos, rope_sin)


def _fourier_kernel(ct_ref, snt_ref, g_ref, o_ref):
    acc = _dot(ct_ref[...], g_ref[:, :FOURIER_WIDTH])
    acc = acc + _dot(snt_ref[...], g_ref[:, FOURIER_WIDTH:])
    o_ref[...] = acc.astype(BF16)


def _fourier(g, ct, snt, B, S):
    tm = _tiles(S)["tm_dft"]
    n_row = S // tm
    return pl.pallas_call(
        _fourier_kernel,
        grid=(n_row, B),
        in_specs=[
            pl.BlockSpec((tm, S), lambda i, b: (i, 0), pipeline_mode=pl.Buffered(1)),
            pl.BlockSpec((tm, S), lambda i, b: (i, 0), pipeline_mode=pl.Buffered(1)),
            pl.BlockSpec((S, 2 * FOURIER_WIDTH), lambda i, b: (b, 0)),
        ],
        out_specs=pl.BlockSpec((tm, FOURIER_WIDTH), lambda i, b: (b * n_row + i, 0)),
        out_shape=jax.ShapeDtypeStruct((B * S, FOURIER_WIDTH), BF16),
        compiler_params=pltpu.CompilerParams(
            dimension_semantics=("parallel", "parallel"), vmem_limit_bytes=48 * MIB),
        name="fourier",
    )(ct, snt, g)


def _attn_kernel(lam_ref, qt_ref, k_ref, vt_ref, g_ref, o_ref, e_ref, m_ref, *s_refs, tkp, lambda_init):
    tq = qt_ref.shape[1]
    width = 2 * tq
    S = k_ref.shape[0]
    n_p = S // tkp
    groups = tkp // SUBLANES
    ring = len(s_refs)
    lp = lam_ref[...]
    lam = (jnp.exp(jnp.sum(lp[0:1, :] * lp[1:2, :], axis=-1, keepdims=True))
           - jnp.exp(jnp.sum(lp[2:3, :] * lp[3:4, :], axis=-1, keepdims=True))
           + lambda_init)
    qt = qt_ref[...]
    row = lax.broadcasted_iota(jnp.int32, qt.shape, 0)
    zero = jnp.zeros((), BF16)
    qt_both = jnp.concatenate(
        [jnp.where(row < DIFF_HEAD_DIM, qt, zero), jnp.where(row >= DIFF_HEAD_DIM, qt, zero)], axis=1)

    m_class = jnp.full((SUBLANES, width), -jnp.inf, F32)
    m_exp = m_class
    packed = 2 * SUBLANES
    l_class = jnp.zeros((SUBLANES, width), F32)
    for p in range(n_p + 1):
        if p < n_p:
            s = _dot(k_ref[p * tkp:(p + 1) * tkp, :], qt_both)
            s_refs[p % ring][...] = s
            m_class = jnp.maximum(m_class, jnp.max(s.reshape(groups, SUBLANES, width), axis=0))
            m_ref[p] = m_class
        if p >= 1:
            j = p - 1
            m_new = m_ref[j]
            m_tile = jnp.concatenate([m_new, m_new], axis=0)
            d = (s_refs[j % ring][...].reshape(tkp // packed, packed, width) - m_tile[None]).astype(BF16)
            e = jnp.exp2(d)
            e_ref[j * tkp:(j + 1) * tkp, :] = e.reshape(tkp, width)
            part = e
            while part.shape[0] > 1:
                half_n = part.shape[0] // 2
                part = part[:half_n] + part[half_n:]
            part = part[0].astype(F32)
            l_class = (l_class * jnp.exp2(m_exp - m_new)
                       + part[:SUBLANES, :] + part[SUBLANES:, :])
            m_exp = m_new

    m_all = jnp.max(m_exp, axis=0, keepdims=True)
    l_all = jnp.sum(l_class * jnp.exp2(m_exp - m_all), axis=0, keepdims=True)
    scale = jnp.concatenate([1.0 / l_all[:, :tq], lam / l_all[:, tq:]], axis=1)
    acc = jnp.zeros((DIFF_V_DIM, tq), F32)
    for p in range(n_p):
        f = jnp.exp2(m_ref[p] - m_all) * scale
        f = jnp.concatenate([f, f], axis=0).astype(BF16)
        ew = e_ref[p * tkp:(p + 1) * tkp, :].reshape(tkp // packed, packed, width) * f[None]
        a = (ew[:, :, :tq] - ew[:, :, tq:]).reshape(tkp, tq)
        acc = acc + _dot(vt_ref[:, p * tkp:(p + 1) * tkp], a)

    o = acc.T
    o = o * lax.rsqrt(jnp.mean(o * o, axis=-1, keepdims=True) + RMS_EPS) * g_ref[...]
    o_ref[...] = (o * (1.0 - lambda_init)).astype(BF16)


def _attn(lam_params, qt, k, vat, subln_g, B, S, lambda_init):
    t = _tiles(S)
    tq, tkp = t["tq"], t["tkp"]
    n_q = S // tq
    kern = functools.partial(_attn_kernel, tkp=tkp, lambda_init=lambda_init)
    return pl.pallas_call(
        kern,
        grid=(B, DIFF_HEADS, n_q),
        in_specs=[
            pl.BlockSpec((4, DIFF_HEAD_DIM), lambda b, h, i: (0, 0)),
            pl.BlockSpec((DIFF_V_DIM, tq), lambda b, h, i: (h, b * n_q + i)),
            pl.BlockSpec((S, DIFF_V_DIM), lambda b, h, i: (b, h)),
            pl.BlockSpec((DIFF_V_DIM, S), lambda b, h, i: (h, b)),
            pl.BlockSpec((1, DIFF_V_DIM), lambda b, h, i: (0, 0)),
        ],
        out_specs=pl.BlockSpec((tq, DIFF_V_DIM), lambda b, h, i: (b * n_q + i, h)),
        out_shape=jax.ShapeDtypeStruct((B * S, DIFF_V_WIDTH), BF16),
        scratch_shapes=[pltpu.VMEM((S, 2 * tq), BF16),
                        pltpu.VMEM((S // tkp, SUBLANES, 2 * tq), F32)]
        + [pltpu.VMEM((tkp, 2 * tq), F32)] * min(SCORE_LEAD, S // tkp),
        compiler_params=pltpu.CompilerParams(
            dimension_semantics=("parallel", "parallel", "parallel"), vmem_limit_bytes=48 * MIB),
        name="attn",
    )(lam_params, qt, k, vat, subln_g)


def _merge_kernel(*refs, with_router):
    if with_router:
        (x_ref, fo_ref, so_ref, do_ref, wg_ref, wf_ref, ws_ref, wd_ref, wo_ref, lg_ref, lb_ref,
         wr_ref, x1_out, route_out, counts_out, base_ref) = refs
    else:
        (x_ref, fo_ref, so_ref, do_ref, wg_ref, wf_ref, ws_ref, wd_ref, wo_ref, lg_ref, lb_ref,
         x1_out) = refs
    x = x_ref[...]
    xb = x.astype(BF16)
    merged = None
    for j, (br_ref, wb_ref) in enumerate(((fo_ref, wf_ref), (so_ref, ws_ref), (do_ref, wd_ref))):
        gate = jax.nn.sigmoid(_dot(xb, wg_ref[:, j * D_MODEL:(j + 1) * D_MODEL]))
        term = gate * _dot(br_ref[...], wb_ref[...])
        merged = term if merged is None else merged + term
    y = _dot(merged.astype(BF16), wo_ref[...])
    x1 = _layer_norm(ALPHA * x + y, lg_ref[...], lb_ref[...])
    x1_out[...] = x1

    if with_router:
        tm = x.shape[0]
        logits = _dot(x1.astype(BF16), wr_ref[...])
        lane = lax.broadcasted_iota(jnp.int32, (tm, ROUTER_PAD), 1)
        neg = jnp.float32(-jnp.inf)
        lg = jnp.where(lane < N_EXPERTS, logits, neg)
        v1 = jnp.max(lg, axis=-1, keepdims=True)
        i1 = jnp.min(jnp.where(lg == v1, lane, ROUTER_PAD), axis=-1, keepdims=True)
        lg2 = jnp.where(lane == i1, neg, lg)
        v2 = jnp.max(lg2, axis=-1, keepdims=True)
        i2 = jnp.min(jnp.where(lg2 == v2, lane, ROUTER_PAD), axis=-1, keepdims=True)
        e2 = jnp.exp(v2 - v1)
        den = 1.0 + e2

        @pl.when(pl.program_id(0) == 0)
        def _():
            base_ref[...] = jnp.zeros_like(base_ref)

        pick1 = lane == i1
        pick2 = lane == i2
        sel = jnp.logical_or(pick1, pick2)
        r_id = lax.broadcasted_iota(jnp.int32, (tm, tm), 0)
        c_id = lax.broadcasted_iota(jnp.int32, (tm, tm), 1)
        ltri = (c_id < r_id).astype(BF16)
        base = base_ref[0:1, :]
        rank = base + _dot(ltri, sel.astype(BF16))
        r1 = jnp.sum(jnp.where(pick1, rank, 0.0), axis=-1, keepdims=True)
        r2 = jnp.sum(jnp.where(pick2, rank, 0.0), axis=-1, keepdims=True)
        new_base = base + jnp.sum(sel.astype(F32), axis=0, keepdims=True)
        base_ref[...] = jnp.broadcast_to(new_base, base_ref.shape)
        counts_out[...] = jnp.broadcast_to(new_base, counts_out.shape)
        fields = (1.0 / den, e2 / den, i1.astype(F32), i2.astype(F32), r1, r2)
        route = jnp.zeros((tm, ROUTER_PAD), F32)
        for n, val in enumerate(fields):
            route = jnp.where(lane == n, val, route)
        route_out[...] = route


def _merge(x, fo, so, do, w_g, w_f, w_s, w_d, w_o, ln_g, ln_b, w_router, S):
    T = x.shape[0]
    tm = _tiles(S)["tm_in"]
    with_router = w_router is not None
    full = lambda shape: pl.BlockSpec(shape, lambda i: (0,) * len(shape))
    row = lambda width: pl.BlockSpec((tm, width), lambda i: (i, 0))
    in_specs = [
        row(D_MODEL), row(FOURIER_WIDTH), row(SGU_WIDTH), row(DIFF_V_WIDTH),
        full((D_MODEL, N_BRANCHES * D_MODEL)),
        full((FOURIER_WIDTH, D_MODEL)), full((SGU_WIDTH, D_MODEL)), full((DIFF_V_WIDTH, D_MODEL)),
        full((D_MODEL, D_MODEL)), full((1, D_MODEL)), full((1, D_MODEL)),
    ]
    args = [x, fo, so, do, w_g, w_f, w_s, w_d, w_o, ln_g, ln_b]
    out_specs = [row(D_MODEL)]
    out_shape = [jax.ShapeDtypeStruct((T, D_MODEL), F32)]
    scratch = []
    if with_router:
        in_specs.append(full((D_MODEL, ROUTER_PAD)))
        args.append(w_router)
        out_specs += [row(ROUTER_PAD), full((SUBLANES, ROUTER_PAD))]
        out_shape += [jax.ShapeDtypeStruct((T, ROUTER_PAD), F32),
                      jax.ShapeDtypeStruct((SUBLANES, ROUTER_PAD), F32)]
        scratch.append(pltpu.VMEM((SUBLANES, ROUTER_PAD), F32))
    outs = pl.pallas_call(
        functools.partial(_merge_kernel, with_router=with_router),
        grid=(T // tm,),
        in_specs=in_specs,
        out_specs=out_specs,
        out_shape=out_shape,
        scratch_shapes=scratch,
        compiler_params=pltpu.CompilerParams(
            dimension_semantics=("arbitrary" if with_router else "parallel",),
            vmem_limit_bytes=56 * MIB),
        name="merge_router" if with_router else "merge",
    )(*args)
    return (outs[0], outs[1], outs[2]) if with_router else (outs[0], None, None)


def _swiglu(xb, wg_ref, wu_ref, wd_ref, tf):
    y = None
    for f in range(D_FF_PAD // tf):
        cols = slice(f * tf, (f + 1) * tf)
        h = jax.nn.silu(_dot(xb, wg_ref[:, cols])) * _dot(xb, wu_ref[:, cols])
        part = _dot(h.astype(BF16), wd_ref[cols, :])
        y = part if y is None else y + part
    return y


def _ffn_kernel(x_ref, wg_ref, wu_ref, wd_ref, lg_ref, lb_ref, o_ref, *, tf):
    x = x_ref[...]
    ff = _swiglu(x.astype(BF16), wg_ref, wu_ref, wd_ref, tf)
    o_ref[...] = _layer_norm(ALPHA * x + ff, lg_ref[...], lb_ref[...])


def _ffn(x, w_gate, w_up, w_down, ln_g, ln_b, S):
    T = x.shape[0]
    t = _tiles(S)
    tm, tf = t["tm_in"], t["tf"]
    resident = lambda shape: pl.BlockSpec(shape, lambda i: (0, 0), pipeline_mode=pl.Buffered(1))
    return pl.pallas_call(
        functools.partial(_ffn_kernel, tf=tf),
        grid=(T // tm,),
        in_specs=[
            pl.BlockSpec((tm, D_MODEL), lambda i: (i, 0)),
            resident((D_MODEL, D_FF_PAD)), resident((D_MODEL, D_FF_PAD)), resident((D_FF_PAD, D_MODEL)),
            pl.BlockSpec((1, D_MODEL), lambda i: (0, 0)),
            pl.BlockSpec((1, D_MODEL), lambda i: (0, 0)),
        ],
        out_specs=pl.BlockSpec((tm, D_MODEL), lambda i: (i, 0)),
        out_shape=jax.ShapeDtypeStruct((T, D_MODEL), F32),
        compiler_params=pltpu.CompilerParams(
            dimension_semantics=("parallel",), vmem_limit_bytes=56 * MIB),
        name="ffn_dense",
    )(x, w_gate, w_up, w_down, ln_g, ln_b)


def _route_tables(route, counts, T):
    R = MOE_ROWS
    as_int = lambda n: route[:, n].astype(jnp.int32)
    i1, i2, r1, r2 = as_int(2), as_int(3), as_int(4), as_int(5)
    cnt = counts[0, :N_EXPERTS].astype(jnp.int32)
    padded = (cnt + R - 1) // R * R
    ends = jnp.cumsum(padded)
    off = ends - padded
    n_tiles = TOP_K * T // R + N_EXPERTS
    tile_expert = jnp.minimum(
        jnp.searchsorted(ends, jnp.arange(n_tiles, dtype=jnp.int32) * R, side="right"),
        N_EXPERTS - 1).astype(jnp.int32)
    pad_start = off + cnt
    pad_n = (padded - cnt).at[N_EXPERTS - 1].set(n_tiles * R - pad_start[N_EXPERTS - 1])
    return dict(
        pos1=jnp.take(off, i1) + r1, pos2=jnp.take(off, i2) + r2,
        tile_expert=tile_expert, n_used=(ends[-1:] // R).astype(jnp.int32),
        pad=jnp.concatenate([pad_start, pad_n]).astype(jnp.int32), n_tiles=n_tiles)


def _dispatch_kernel(pad_ref, x_ref, p1_ref, p2_ref, xs_ref, zero_ref, sem):
    tm = x_ref.shape[0]

    def row_copy(src_ref, src_row, dst_row):
        return pltpu.make_async_copy(src_ref.at[pl.ds(src_row, 1)], xs_ref.at[pl.ds(dst_row, 1)], sem)

    @pl.when(pl.program_id(0) == 0)
    def _():
        zero_ref[...] = jnp.zeros_like(zero_ref)
        for e in range(N_EXPERTS):
            start = pad_ref[e]
            n = pad_ref[N_EXPERTS + e]

            def issue_pad(k, c):
                row_copy(zero_ref, 0, start + k).start()
                return c

            def drain_pad(k, c):
                row_copy(zero_ref, 0, start + k).wait()
                return c

            lax.fori_loop(0, n, issue_pad, 0)
            lax.fori_loop(0, n, drain_pad, 0)

    def issue(t, c):
        row_copy(x_ref, t, p1_ref[0, 0, t]).start(priority=0)
        row_copy(x_ref, t, p2_ref[0, 0, t]).start(priority=1)
        return c

    def drain(t, c):
        row_copy(x_ref, t, p1_ref[0, 0, t]).wait()
        row_copy(x_ref, t, p2_ref[0, 0, t]).wait()
        return c

    lax.fori_loop(0, tm, issue, 0, unroll=8)
    lax.fori_loop(0, tm, drain, 0, unroll=8)


def _dispatch(x, tabs, S):
    T = x.shape[0]
    tm = _tiles(S)["tm_in"]
    n_rows = tabs["n_tiles"] * MOE_ROWS
    idx_spec = pl.BlockSpec((1, 1, tm), lambda i, pad: (i, 0, 0), memory_space=pltpu.SMEM)
    return pl.pallas_call(
        _dispatch_kernel,
        grid_spec=pltpu.PrefetchScalarGridSpec(
            num_scalar_prefetch=1,
            grid=(T // tm,),
            in_specs=[pl.BlockSpec((tm, D_MODEL), lambda i, pad: (i, 0)), idx_spec, idx_spec],
            out_specs=pl.BlockSpec(memory_space=pl.ANY),
            scratch_shapes=[pltpu.VMEM((SUBLANES, D_MODEL), F32), pltpu.SemaphoreType.DMA(())],
        ),
        out_shape=jax.ShapeDtypeStruct((n_rows, D_MODEL), F32),
        compiler_params=pltpu.CompilerParams(dimension_semantics=("arbitrary",)),
        name="moe_dispatch",
    )(tabs["pad"], x, tabs["pos1"].reshape(T // tm, 1, tm), tabs["pos2"].reshape(T // tm, 1, tm))


def _gmm_kernel(te_ref, nu_ref, x_ref, wg_ref, wu_ref, wd_ref, o_ref, *, tf):
    used = pl.program_id(0) < nu_ref[0]

    @pl.when(jnp.logical_not(used))
    def _():
        o_ref[...] = jnp.zeros_like(o_ref)

    @pl.when(used)
    def _():
        o_ref[...] = _swiglu(x_ref[...].astype(BF16), wg_ref.at[0], wu_ref.at[0], wd_ref.at[0], tf)


def _gmm(xs, tabs, w_gate, w_up, w_down, S):
    R = MOE_ROWS
    tf = _tiles(S)["tf"]
    row_spec = pl.BlockSpec((R, D_MODEL), lambda j, te, nu: (j, 0))
    w_spec = lambda shape: pl.BlockSpec(shape, lambda j, te, nu: (te[j], 0, 0),
                                        pipeline_mode=pl.Buffered(1))
    return pl.pallas_call(
        functools.partial(_gmm_kernel, tf=tf),
        grid_spec=pltpu.PrefetchScalarGridSpec(
            num_scalar_prefetch=2,
            grid=(tabs["n_tiles"],),
            in_specs=[row_spec, w_spec((1, D_MODEL, D_FF_PAD)), w_spec((1, D_MODEL, D_FF_PAD)),
                      w_spec((1, D_FF_PAD, D_MODEL))],
            out_specs=row_spec,
        ),
        out_shape=jax.ShapeDtypeStruct(xs.shape, F32),
        compiler_params=pltpu.CompilerParams(
            dimension_semantics=("arbitrary",), vmem_limit_bytes=56 * MIB),
        name="moe_gmm",
    )(tabs["tile_expert"], tabs["n_used"], xs, w_gate, w_up, w_down)


def _combine_kernel(x_ref, route_ref, p1_ref, p2_ref, lg_ref, lb_ref, ys_ref, o_ref, y1_ref, y2_ref, sem):
    tm = x_ref.shape[0]

    def row_copies(t):
        return (pltpu.make_async_copy(ys_ref.at[pl.ds(p1_ref[0, 0, t], 1)], y1_ref.at[pl.ds(t, 1)], sem),
                pltpu.make_async_copy(ys_ref.at[pl.ds(p2_ref[0, 0, t], 1)], y2_ref.at[pl.ds(t, 1)], sem))

    def issue(t, c):
        for priority, cp in enumerate(row_copies(t)):
            cp.start(priority=priority)
        return c

    def drain(t, c):
        for cp in row_copies(t):
            cp.wait()
        return c

    lax.fori_loop(0, tm, issue, 0, unroll=8)
    lax.fori_loop(0, tm, drain, 0, unroll=8)
    route = route_ref[...]
    ff = route[:, 0:1] * y1_ref[...] + route[:, 1:2] * y2_ref[...]
    o_ref[...] = _layer_norm(ALPHA * x_ref[...] + ff, lg_ref[...], lb_ref[...])


def _combine(x, route, tabs, ys, ln_g, ln_b, S):
    T = x.shape[0]
    tm = _tiles(S)["tm_comb"]
    idx_spec = pl.BlockSpec((1, 1, tm), lambda i: (i, 0, 0), memory_space=pltpu.SMEM)
    return pl.pallas_call(
        _combine_kernel,
        grid=(T // tm,),
        in_specs=[
            pl.BlockSpec((tm, D_MODEL), lambda i: (i, 0)),
            pl.BlockSpec((tm, ROUTER_PAD), lambda i: (i, 0)),
            idx_spec, idx_spec,
            pl.BlockSpec((1, D_MODEL), lambda i: (0, 0)),
            pl.BlockSpec((1, D_MODEL), lambda i: (0, 0)),
            pl.BlockSpec(memory_space=pl.ANY),
        ],
        out_specs=pl.BlockSpec((tm, D_MODEL), lambda i: (i, 0)),
        out_shape=jax.ShapeDtypeStruct((T, D_MODEL), F32),
        scratch_shapes=[pltpu.VMEM((tm, D_MODEL), F32), pltpu.VMEM((tm, D_MODEL), F32),
                        pltpu.SemaphoreType.DMA(())],
        compiler_params=pltpu.CompilerParams(dimension_semantics=("arbitrary",)),
        name="moe_combine",
    )(x, route, tabs["pos1"].reshape(T // tm, 1, tm), tabs["pos2"].reshape(T // tm, 1, tm), ln_g, ln_b, ys)


def _rope_tables(S):
    half = DIFF_HEAD_DIM // 2
    inv = ROPE_THETA ** (-jnp.arange(0, DIFF_HEAD_DIM, 2, dtype=F32) / DIFF_HEAD_DIM)
    ang = jnp.arange(S, dtype=F32)[:, None] * inv[None, :]
    cos, sin = jnp.cos(ang), jnp.sin(ang)
    reps = LANES // DIFF_HEAD_DIM
    cos_t = jnp.tile(jnp.concatenate([cos, cos], axis=1), (1, reps))
    sin_t = jnp.tile(jnp.concatenate([-sin, sin], axis=1), (1, reps))
    assert cos_t.shape == (S, LANES) and half * 2 == DIFF_HEAD_DIM
    return cos_t, sin_t


def _dft_tables(S):
    split = math.gcd(S, DFT_ROW_SPLIT)
    k = jnp.arange(S, dtype=jnp.int32)[None, :]

    def thin(rows):
        ang = ((rows[:, None] * k) % S).astype(F32) * (2.0 * math.pi / S)
        return jnp.cos(ang), jnp.sin(ang)

    ch, sh = thin(jnp.arange(S // split, dtype=jnp.int32) * split)
    cl, sl = thin(jnp.arange(split, dtype=jnp.int32))
    scale = S ** -0.5
    cos = (ch[:, None, :] * cl[None, :, :] - sh[:, None, :] * sl[None, :, :]) * scale
    sin = (sh[:, None, :] * cl[None, :, :] + ch[:, None, :] * sl[None, :, :]) * -scale
    return cos.reshape(S, S).astype(BF16), sin.reshape(S, S).astype(BF16)


def _dft64_table():
    j = jnp.arange(FOURIER_GROUP, dtype=jnp.int32)
    ang = ((j[:, None] * j[None, :]) % FOURIER_GROUP).astype(F32) * (2.0 * math.pi / FOURIER_GROUP)
    scale = FOURIER_GROUP ** -0.5
    eye = jnp.eye(N_FOURIER_GROUPS, dtype=F32)
    return jnp.concatenate([jnp.kron(eye, jnp.cos(ang) * scale),
                            jnp.kron(eye, jnp.sin(ang) * scale)], axis=1).astype(BF16)


def _pad_ff(w, axis):
    pad = [(0, 0)] * w.ndim
    pad[axis] = (0, D_FF_PAD - D_FF)
    return jnp.pad(w, pad).astype(BF16)


def _prepare(w_in,w_fourier, w_sgu, w_diff, w_out, vn_g, vn_b, sgu_w, sgu_b,
             lam_q1, lam_k1, lam_q2, lam_k2, subln_g, ln1_g, ln1_b, ln2_g, ln2_b,
             ffn_w_gate, ffn_w_up, ffn_w_down, w_router, moe_w_gate, moe_w_up, moe_w_down):
    w_in_b = w_in.astype(BF16)
    row = lambda a: a.astype(F32)[:, None, :]
    return dict(
        w_a=w_in_b[:, :, :OFF_G], w_g=w_in_b[:, :, OFF_G:],
        w_f=w_fourier.astype(BF16), w_s=w_sgu.astype(BF16), w_d=w_diff.astype(BF16),
        w_o=w_out.astype(BF16),
        vn_g=row(vn_g), vn_b=row(vn_b),
        sgu_wcat=jnp.transpose(sgu_w, (0, 2, 1, 3)).reshape(DEPTH, CHUNK, SGU_HEADS * CHUNK).astype(BF16),
        sgu_bias=jnp.repeat(jnp.transpose(sgu_b, (0, 2, 1)).astype(F32), SGU_HEAD_DIM, axis=2),
        lam=jnp.stack([lam_q1, lam_k1, lam_q2, lam_k2], axis=1).astype(F32),
        subln_g=row(subln_g), ln1_g=row(ln1_g), ln1_b=row(ln1_b), ln2_g=row(ln2_g), ln2_b=row(ln2_b),
        ffn_gate=_pad_ff(ffn_w_gate, 2), ffn_up=_pad_ff(ffn_w_up, 2), ffn_down=_pad_ff(ffn_w_down, 1),
        w_router=jnp.pad(w_router, ((0, 0), (0, 0), (0, ROUTER_PAD - N_EXPERTS))).astype(BF16),
        moe_gate=_pad_ff(moe_w_gate, 3), moe_up=_pad_ff(moe_w_up, 3), moe_down=_pad_ff(moe_w_down, 2),
        dft64=_dft64_table(),
    )


def _trunk(x3, p):
    B, S, _ = x3.shape
    x = x3.reshape(B * S, D_MODEL)
    rope_cos, rope_sin = _rope_tables(S)
    ct, snt = _dft_tables(S)
    for l in range(DEPTH):
        lambda_init = 0.8 - 0.6 * math.exp(-0.3 * l)
        g, so, qt, k, vat = _inproj(x, p["w_a"][l], p["dft64"], p["vn_g"][l], p["vn_b"][l],
                                    p["sgu_wcat"][l], p["sgu_bias"][l], rope_cos, rope_sin, S)
        fo = _fourier(g, ct, snt, B, S)
        do = _attn(p["lam"][l], qt, k, vat, p["subln_g"][l], B, S, lambda_init)
        j = l // 2
        moe = l % 2 == 1
        x1, route, counts = _merge(x, fo, so, do, p["w_g"][l], p["w_f"][l], p["w_s"][l], p["w_d"][l],
                                   p["w_o"][l], p["ln1_g"][l], p["ln1_b"][l],
                                   p["w_router"][j] if moe else None, S)
        if moe:
            tabs = _route_tables(route, counts, B * S)
            xs = _dispatch(x1, tabs, S)
            ys = _gmm(xs, tabs, p["moe_gate"][j], p["moe_up"][j], p["moe_down"][j], S)
            x = _combine(x1, route, tabs, ys, p["ln2_g"][l], p["ln2_b"][l], S)
        else:
            x = _ffn(x1, p["ffn_gate"][j], p["ffn_up"][j], p["ffn_down"][j],
                     p["ln2_g"][l], p["ln2_b"][l], S)
    return x.reshape(B, S, D_MODEL)


def kernel(x_prompt, x_sample, w_in, w_fourier, w_sgu, w_diff, w_out, vn_g, vn_b, sgu_w, sgu_b,
           lam_q1, lam_k1, lam_q2, lam_k2, subln_g, ln1_g, ln1_b, ln2_g, ln2_b,
           ffn_w_gate, ffn_w_up, ffn_w_down, w_router, moe_w_gate, moe_w_up, moe_w_down):
    p = _prepare(w_in, w_fourier, w_sgu, w_diff, w_out, vn_g, vn_b, sgu_w, sgu_b,
                 lam_q1, lam_k1, lam_q2, lam_k2, subln_g, ln1_g, ln1_b, ln2_g, ln2_b,
                 ffn_w_gate, ffn_w_up, ffn_w_down, w_router, moe_w_gate, moe_w_up, moe_w_down)
    return (_trunk(x_prompt, p), _trunk(x_sample, p))
```

```python
import functools
import math

import jax
import jax.numpy as jnp
from jax import lax
from jax.experimental import pallas as pl
from jax.experimental.pallas import tpu as pltpu

F32 = jnp.float32
BF16 = jnp.bfloat16

D_MODEL = 1024
DEPTH = 2
N_FOURIER_GROUPS = 4
FOURIER_GROUP = 64
FOURIER_WIDTH = N_FOURIER_GROUPS * FOURIER_GROUP
SGU_HEADS = 4
SGU_HEAD_DIM = 64
SGU_WIDTH = SGU_HEADS * SGU_HEAD_DIM
CHUNK = 128
DIFF_HEADS = 4
DIFF_HEAD_DIM = 64
DIFF_V_DIM = 2 * DIFF_HEAD_DIM
DIFF_QK_WIDTH = DIFF_HEADS * 2 * DIFF_HEAD_DIM
DIFF_V_WIDTH = DIFF_HEADS * DIFF_V_DIM
ROPE_THETA = 10000.0
N_BRANCHES = 3
D_FF = 2752
N_EXPERTS = 8
ALPHA = (2 * DEPTH) ** 0.25
LN_EPS = 1e-5
RMS_EPS = 1e-5

OFF_F = 0
OFF_U = OFF_F + FOURIER_WIDTH
OFF_V = OFF_U + SGU_WIDTH
OFF_Q = OFF_V + SGU_WIDTH
OFF_K = OFF_Q + DIFF_QK_WIDTH
OFF_VA = OFF_K + DIFF_QK_WIDTH
OFF_G = OFF_VA + DIFF_V_WIDTH
IN_WIDTH = OFF_G + N_BRANCHES * D_MODEL

TOP_K = 2
LANES = 128
SUBLANES = 8
MXU_WIDTH = 256
MOE_ROWS = 512
DFT_ROW_SPLIT = 64
SCORE_LEAD = 2
D_FF_PAD = 2816
ROUTER_PAD = LANES
MIB = 1024 * 1024


def _tiles(S):
    t = dict(
        tm_in=min(512, S),
        tm_comb=min(256, S),
        tm_dft=min(512, S),
        tq=min(512, S),
        tkc=min(128, S),
        tkp=min(512, S),
        tf=D_FF_PAD // 2,
    )
    return t


def _layer_norm(z, g, b):
    mu = jnp.mean(z, axis=-1, keepdims=True)
    zc = z - mu
    var = jnp.mean(zc * zc, axis=-1, keepdims=True)
    return zc * lax.rsqrt(var + LN_EPS) * g + b


def _dot(a, b):
    return jnp.dot(a, b, preferred_element_type=F32)


def _inproj_kernel(x_ref, w_ref, dft_ref, vng_ref, vnb_ref, sguw_ref, sgub_ref, cos_ref, sin_ref,
                   g_out, so_out, qt_out, k_out, vat_out):
    tm = x_ref.shape[0]
    xb = x_ref[...].astype(BF16)

    def proj(a, b):
        return _dot(xb, w_ref[:, a:b])

    f = proj(OFF_F, OFF_U)
    g_out[...] = _dot(f.astype(BF16), dft_ref[...]).astype(BF16)

    u = proj(OFF_U, OFF_V)
    v = proj(OFF_V, OFF_Q)
    vn = _layer_norm(v, vng_ref[...], vnb_ref[...]).astype(BF16)
    row_head = lax.broadcasted_iota(jnp.int32, (SGU_HEADS * CHUNK, SGU_WIDTH), 0) // CHUNK
    col_head = lax.broadcasted_iota(jnp.int32, (SGU_HEADS * CHUNK, SGU_WIDTH), 1) // SGU_HEAD_DIM
    head_mask = row_head == col_head
    for c in range(tm // CHUNK):
        rows = slice(c * CHUNK, (c + 1) * CHUNK)
        vb = vn[rows, :]
        rhs = jnp.where(head_mask, jnp.concatenate([vb] * SGU_HEADS, axis=0), jnp.zeros((), BF16))
        mixed = _dot(sguw_ref[...], rhs) + sgub_ref[...]
        so_out[rows, :] = (u[rows, :] * mixed).astype(BF16)

    cs = cos_ref[...]
    sn = sin_ref[...]
    lane = lax.broadcasted_iota(jnp.int32, (tm, LANES), 1)
    first_half = (lane % DIFF_HEAD_DIM) < (DIFF_HEAD_DIM // 2)

    def rope(t):
        outs = []
        for j in range(DIFF_QK_WIDTH // LANES):
            tc = t[:, j * LANES:(j + 1) * LANES]
            rot = jnp.where(first_half,
                            pltpu.roll(tc, LANES - DIFF_HEAD_DIM // 2, 1),
                            pltpu.roll(tc, DIFF_HEAD_DIM // 2, 1))
            outs.append(tc * cs + rot * sn)
        return jnp.concatenate(outs, axis=1)

    q = rope(proj(OFF_Q, OFF_K)) * (DIFF_HEAD_DIM ** -0.5)
    qt_out[...] = q.T.astype(BF16)
    k_out[...] = rope(proj(OFF_K, OFF_VA)).astype(BF16)
    vat_out[...] = proj(OFF_VA, OFF_G).T.astype(BF16)


def _inproj(x, w_a, dft64, vn_g, vn_b, sgu_wcat, sgu_bias, rope_cos, rope_sin, S):
    T = x.shape[0]
    tm = _tiles(S)["tm_in"]
    n_pos = S // tm
    full = lambda shape: pl.BlockSpec(shape, lambda i: (0,) * len(shape))
    return pl.pallas_call(
        _inproj_kernel,
        grid=(T // tm,),
        in_specs=[
            pl.BlockSpec((tm, D_MODEL), lambda i: (i, 0)),
            full((D_MODEL, OFF_G)),
            full((FOURIER_WIDTH, 2 * FOURIER_WIDTH)),
            full((1, SGU_WIDTH)),
            full((1, SGU_WIDTH)),
            full((CHUNK, SGU_HEADS * CHUNK)),
            full((CHUNK, SGU_WIDTH)),
            pl.BlockSpec((tm, LANES), lambda i: (i % n_pos, 0)),
            pl.BlockSpec((tm, LANES), lambda i: (i % n_pos, 0)),
        ],
        out_specs=[
            pl.BlockSpec((tm, 2 * FOURIER_WIDTH), lambda i: (i, 0)),
            pl.BlockSpec((tm, SGU_WIDTH), lambda i: (i, 0)),
            pl.BlockSpec((DIFF_QK_WIDTH, tm), lambda i: (0, i)),
            pl.BlockSpec((tm, DIFF_QK_WIDTH), lambda i: (i, 0)),
            pl.BlockSpec((DIFF_V_WIDTH, tm), lambda i: (0, i)),
        ],
        out_shape=[
            jax.ShapeDtypeStruct((T, 2 * FOURIER_WIDTH), BF16),
            jax.ShapeDtypeStruct((T, SGU_WIDTH), BF16),
            jax.ShapeDtypeStruct((DIFF_QK_WIDTH, T), BF16),
            jax.ShapeDtypeStruct((T, DIFF_QK_WIDTH), BF16),
            jax.ShapeDtypeStruct((DIFF_V_WIDTH, T), BF16),
        ],
        compiler_params=pltpu.CompilerParams(
            dimension_semantics=("parallel",), vmem_limit_bytes=48 * MIB),
        name="inproj",
    )(x, w_a, dft64, vn_g, vn_b, sgu_wcat, sgu_bias, rope_cos, rope_sin)


def _fourier_kernel(ct_ref, snt_ref, g_ref, o_ref):
    acc = _dot(ct_ref[...], g_ref[:, :FOURIER_WIDTH])
    acc = acc + _dot(snt_ref[...], g_ref[:, FOURIER_WIDTH:])
    o_ref[...] = acc.astype(BF16)


def _fourier(g, ct, snt, B, S):
    tm = _tiles(S)["tm_dft"]
    n_row = S // tm
    return pl.pallas_call(
        _fourier_kernel,
        grid=(n_row, B),
        in_specs=[
            pl.BlockSpec((tm, S), lambda i, b: (i, 0), pipeline_mode=pl.Buffered(1)),
            pl.BlockSpec((tm, S), lambda i, b: (i, 0), pipeline_mode=pl.Buffered(1)),
            pl.BlockSpec((S, 2 * FOURIER_WIDTH), lambda i, b: (b, 0)),
        ],
        out_specs=pl.BlockSpec((tm, FOURIER_WIDTH), lambda i, b: (b * n_row + i, 0)),
        out_shape=jax.ShapeDtypeStruct((B * S, FOURIER_WIDTH), BF16),
        compiler_params=pltpu.CompilerParams(
            dimension_semantics=("parallel", "parallel"), vmem_limit_bytes=48 * MIB),
        name="fourier",
    )(ct, snt, g)


def _attn_kernel(lam_ref, qt_ref, k_ref, vt_ref, g_ref, o_ref, s_ref, e_ref, m_ref, *, tkp, lambda_init):
    tq = qt_ref.shape[1]
    width = 2 * tq
    S = k_ref.shape[0]
    n_p = S // tkp
    groups = tkp // SUBLANES
    ring = s_ref.shape[0]
    lp = lam_ref[...]
    lam = (jnp.exp(jnp.sum(lp[0:1, :] * lp[1:2, :], axis=-1, keepdims=True))
           - jnp.exp(jnp.sum(lp[2:3, :] * lp[3:4, :], axis=-1, keepdims=True))
           + lambda_init)
    qt = qt_ref[...]
    row = lax.broadcasted_iota(jnp.int32, qt.shape, 0)
    zero = jnp.zeros((), BF16)
    qt_both = jnp.concatenate(
        [jnp.where(row < DIFF_HEAD_DIM, qt, zero), jnp.where(row >= DIFF_HEAD_DIM, qt, zero)], axis=1)

    m_class = jnp.full((SUBLANES, width), -jnp.inf, F32)
    m_exp = m_class
    l_class = jnp.zeros((SUBLANES, width), F32)
    l_hist = []
    for p in range(n_p + 1):
        if p < n_p:
            hold = 0
            if p >= ring:
                bits = pltpu.bitcast(l_hist[p - ring], jnp.int32)[0, 0]
                half = jnp.int32(16)
                hold = lax.shift_right_logical(lax.shift_right_logical(bits, half), half)
            kc = k_ref[pl.ds(pl.multiple_of(p * tkp + hold, tkp), tkp), :]
            s = _dot(kc, qt_both)
            s_ref[p % ring] = s
            m_class = jnp.maximum(m_class, jnp.max(s.reshape(groups, SUBLANES, width), axis=0))
            m_ref[p] = m_class
        if p >= 1:
            j = p - 1
            m_new = m_ref[j]
            e = jnp.exp(s_ref[j % ring].reshape(groups, SUBLANES, width) - m_new[None])
            e_ref[j * tkp:(j + 1) * tkp, :] = e.reshape(tkp, width).astype(BF16)
            l_class = l_class * jnp.exp(m_exp - m_new) + jnp.sum(e, axis=0)
            m_exp = m_new
            l_hist.append(l_class)

    m_all = jnp.max(m_exp, axis=0, keepdims=True)
    l_all = jnp.sum(l_class * jnp.exp(m_exp - m_all), axis=0, keepdims=True)
    scale = jnp.concatenate([1.0 / l_all[:, :tq], lam / l_all[:, tq:]], axis=1)
    packed = 2 * SUBLANES
    acc = jnp.zeros((DIFF_V_DIM, tq), F32)
    for p in range(n_p):
        f = jnp.exp(m_ref[p] - m_all) * scale
        f = jnp.concatenate([f, f], axis=0).astype(BF16)
        ew = e_ref[p * tkp:(p + 1) * tkp, :].reshape(tkp // packed, packed, width) * f[None]
        a = (ew[:, :, :tq] - ew[:, :, tq:]).reshape(tkp, tq)
        acc = acc + _dot(vt_ref[:, p * tkp:(p + 1) * tkp], a)

    o = acc.T
    o = o * lax.rsqrt(jnp.mean(o * o, axis=-1, keepdims=True) + RMS_EPS) * g_ref[...]
    o_ref[...] = (o * (1.0 - lambda_init)).astype(BF16)


def _attn(lam_params, qt, k, vat, subln_g, B, S, lambda_init):
    t = _tiles(S)
    tq, tkp = t["tq"], t["tkp"]
    n_q = S // tq
    kern = functools.partial(_attn_kernel, tkp=tkp, lambda_init=lambda_init)
    return pl.pallas_call(
        kern,
        grid=(B, DIFF_HEADS, n_q),
        in_specs=[
            pl.BlockSpec((4, DIFF_HEAD_DIM), lambda b, h, i: (0, 0)),
            pl.BlockSpec((DIFF_V_DIM, tq), lambda b, h, i: (h, b * n_q + i)),
            pl.BlockSpec((S, DIFF_V_DIM), lambda b, h, i: (b, h)),
            pl.BlockSpec((DIFF_V_DIM, S), lambda b, h, i: (h, b)),
            pl.BlockSpec((1, DIFF_V_DIM), lambda b, h, i: (0, 0)),
        ],
        out_specs=pl.BlockSpec((tq, DIFF_V_DIM), lambda b, h, i: (b * n_q + i, h)),
        out_shape=jax.ShapeDtypeStruct((B * S, DIFF_V_WIDTH), BF16),
        scratch_shapes=[pltpu.VMEM((min(SCORE_LEAD, S // tkp), tkp, 2 * tq), F32),
                        pltpu.VMEM((S, 2 * tq), BF16),
                        pltpu.VMEM((S // tkp, SUBLANES, 2 * tq), F32)],
        compiler_params=pltpu.CompilerParams(
            dimension_semantics=("parallel", "parallel", "parallel"), vmem_limit_bytes=48 * MIB),
        name="attn",
    )(lam_params, qt, k, vat, subln_g)


def _merge_kernel(*refs, with_router):
    if with_router:
        (x_ref, fo_ref, so_ref, do_ref, wg_ref, wf_ref, ws_ref, wd_ref, wo_ref, lg_ref, lb_ref,
         wr_ref, x1_out, route_out, counts_out, base_ref) = refs
    else:
        (x_ref, fo_ref, so_ref, do_ref, wg_ref, wf_ref, ws_ref, wd_ref, wo_ref, lg_ref, lb_ref,
         x1_out) = refs
    x = x_ref[...]
    xb = x.astype(BF16)
    merged = None
    for j, (br_ref, wb_ref) in enumerate(((fo_ref, wf_ref), (so_ref, ws_ref), (do_ref, wd_ref))):
        gate = jax.nn.sigmoid(_dot(xb, wg_ref[:, j * D_MODEL:(j + 1) * D_MODEL]))
        term = gate * _dot(br_ref[...], wb_ref[...])
        merged = term if merged is None else merged + term
    y = _dot(merged.astype(BF16), wo_ref[...])
    x1 = _layer_norm(ALPHA * x + y, lg_ref[...], lb_ref[...])
    x1_out[...] = x1

    if with_router:
        tm = x.shape[0]
        logits = _dot(x1.astype(BF16), wr_ref[...])
        lane = lax.broadcasted_iota(jnp.int32, (tm, ROUTER_PAD), 1)
        neg = jnp.float32(-jnp.inf)
        lg = jnp.where(lane < N_EXPERTS, logits, neg)
        v1 = jnp.max(lg, axis=-1, keepdims=True)
        i1 = jnp.min(jnp.where(lg == v1, lane, ROUTER_PAD), axis=-1, keepdims=True)
        lg2 = jnp.where(lane == i1, neg, lg)
        v2 = jnp.max(lg2, axis=-1, keepdims=True)
        i2 = jnp.min(jnp.where(lg2 == v2, lane, ROUTER_PAD), axis=-1, keepdims=True)
        e2 = jnp.exp(v2 - v1)
        den = 1.0 + e2

        @pl.when(pl.program_id(0) == 0)
        def _():
            base_ref[...] = jnp.zeros_like(base_ref)

        pick1 = lane == i1
        pick2 = lane == i2
        sel = jnp.logical_or(pick1, pick2)
        r_id = lax.broadcasted_iota(jnp.int32, (tm, tm), 0)
        c_id = lax.broadcasted_iota(jnp.int32, (tm, tm), 1)
        ltri = (c_id < r_id).astype(BF16)
        base = base_ref[0:1, :]
        rank = base + _dot(ltri, sel.astype(BF16))
        r1 = jnp.sum(jnp.where(pick1, rank, 0.0), axis=-1, keepdims=True)
        r2 = jnp.sum(jnp.where(pick2, rank, 0.0), axis=-1, keepdims=True)
        new_base = base + jnp.sum(sel.astype(F32), axis=0, keepdims=True)
        base_ref[...] = jnp.broadcast_to(new_base, base_ref.shape)
        counts_out[...] = jnp.broadcast_to(new_base, counts_out.shape)
        fields = (1.0 / den, e2 / den, i1.astype(F32), i2.astype(F32), r1, r2)
        route = jnp.zeros((tm, ROUTER_PAD), F32)
        for n, val in enumerate(fields):
            route = jnp.where(lane == n, val, route)
        route_out[...] = route


def _merge(x, fo, so, do, w_g, w_f, w_s, w_d, w_o, ln_g, ln_b, w_router, S):
    T = x.shape[0]
    tm = _tiles(S)["tm_in"]
    with_router = w_router is not None
    full = lambda shape: pl.BlockSpec(shape, lambda i: (0,) * len(shape))
    row = lambda width: pl.BlockSpec((tm, width), lambda i: (i, 0))
    in_specs = [
        row(D_MODEL), row(FOURIER_WIDTH), row(SGU_WIDTH), row(DIFF_V_WIDTH),
        full((D_MODEL, N_BRANCHES * D_MODEL)),
        full((FOURIER_WIDTH, D_MODEL)), full((SGU_WIDTH, D_MODEL)), full((DIFF_V_WIDTH, D_MODEL)),
        full((D_MODEL, D_MODEL)), full((1, D_MODEL)), full((1, D_MODEL)),
    ]
    args = [x, fo, so, do, w_g, w_f, w_s, w_d, w_o, ln_g, ln_b]
    out_specs = [row(D_MODEL)]
    out_shape = [jax.ShapeDtypeStruct((T, D_MODEL), F32)]
    scratch = []
    if with_router:
        in_specs.append(full((D_MODEL, ROUTER_PAD)))
        args.append(w_router)
        out_specs += [row(ROUTER_PAD), full((SUBLANES, ROUTER_PAD))]
        out_shape += [jax.ShapeDtypeStruct((T, ROUTER_PAD), F32),
                      jax.ShapeDtypeStruct((SUBLANES, ROUTER_PAD), F32)]
        scratch.append(pltpu.VMEM((SUBLANES, ROUTER_PAD), F32))
    outs = pl.pallas_call(
        functools.partial(_merge_kernel, with_router=with_router),
        grid=(T // tm,),
        in_specs=in_specs,
        out_specs=out_specs,
        out_shape=out_shape,
        scratch_shapes=scratch,
        compiler_params=pltpu.CompilerParams(
            dimension_semantics=("arbitrary" if with_router else "parallel",),
            vmem_limit_bytes=56 * MIB),
        name="merge_router" if with_router else "merge",
    )(*args)
    return (outs[0], outs[1], outs[2]) if with_router else (outs[0], None, None)


def _swiglu(xb, wg_ref, wu_ref, wd_ref, tf):
    y = None
    for f in range(D_FF_PAD // tf):
        cols = slice(f * tf, (f + 1) * tf)
        h = jax.nn.silu(_dot(xb, wg_ref[:, cols])) * _dot(xb, wu_ref[:, cols])
        part = _dot(h.astype(BF16), wd_ref[cols, :])
        y = part if y is None else y + part
    return y


def _ffn_kernel(x_ref, wg_ref, wu_ref, wd_ref, lg_ref, lb_ref, o_ref, *, tf):
    x = x_ref[...]
    ff = _swiglu(x.astype(BF16), wg_ref, wu_ref, wd_ref, tf)
    o_ref[...] = _layer_norm(ALPHA * x + ff, lg_ref[...], lb_ref[...])


def _ffn(x, w_gate, w_up, w_down, ln_g, ln_b, S):
    T = x.shape[0]
    t = _tiles(S)
    tm, tf = t["tm_in"], t["tf"]
    resident = lambda shape: pl.BlockSpec(shape, lambda i: (0, 0), pipeline_mode=pl.Buffered(1))
    return pl.pallas_call(
        functools.partial(_ffn_kernel, tf=tf),
        grid=(T // tm,),
        in_specs=[
            pl.BlockSpec((tm, D_MODEL), lambda i: (i, 0)),
            resident((D_MODEL, D_FF_PAD)), resident((D_MODEL, D_FF_PAD)), resident((D_FF_PAD, D_MODEL)),
            pl.BlockSpec((1, D_MODEL), lambda i: (0, 0)),
            pl.BlockSpec((1, D_MODEL), lambda i: (0, 0)),
        ],
        out_specs=pl.BlockSpec((tm, D_MODEL), lambda i: (i, 0)),
        out_shape=jax.ShapeDtypeStruct((T, D_MODEL), F32),
        compiler_params=pltpu.CompilerParams(
            dimension_semantics=("parallel",), vmem_limit_bytes=56 * MIB),
        name="ffn_dense",
    )(x, w_gate, w_up, w_down, ln_g, ln_b)


def _route_tables(route, counts, T):
    R = MOE_ROWS
    as_int = lambda n: route[:, n].astype(jnp.int32)
    i1, i2, r1, r2 = as_int(2), as_int(3), as_int(4), as_int(5)
    cnt = counts[0, :N_EXPERTS].astype(jnp.int32)
    padded = (cnt + R - 1) // R * R
    ends = jnp.cumsum(padded)
    off = ends - padded
    n_tiles = TOP_K * T // R + N_EXPERTS
    tile_expert = jnp.minimum(
        jnp.searchsorted(ends, jnp.arange(n_tiles, dtype=jnp.int32) * R, side="right"),
        N_EXPERTS - 1).astype(jnp.int32)
    pad_start = off + cnt
    pad_n = (padded - cnt).at[N_EXPERTS - 1].set(n_tiles * R - pad_start[N_EXPERTS - 1])
    return dict(
        pos1=jnp.take(off, i1) + r1, pos2=jnp.take(off, i2) + r2,
        tile_expert=tile_expert, n_used=(ends[-1:] // R).astype(jnp.int32),
        pad=jnp.concatenate([pad_start, pad_n]).astype(jnp.int32), n_tiles=n_tiles)


def _dispatch_kernel(pad_ref, x_ref, p1_ref, p2_ref, xs_ref, x3_ref, zero_ref, sem):
    tm = x_ref.shape[0]
    x3_ref[...] = x_ref[...].reshape(tm, SUBLANES, LANES)

    def row_copy(src_ref, src_row, dst_row):
        return pltpu.make_async_copy(src_ref.at[src_row], xs_ref.at[dst_row], sem)

    @pl.when(pl.program_id(0) == 0)
    def _():
        zero_ref[...] = jnp.zeros_like(zero_ref)
        for e in range(N_EXPERTS):
            start = pad_ref[e]
            n = pad_ref[N_EXPERTS + e]

            def issue_pad(k, c):
                row_copy(zero_ref, 0, start + k).start()
                return c

            def drain_pad(k, c):
                row_copy(zero_ref, 0, start + k).wait()
                return c

            lax.fori_loop(0, n, issue_pad, 0)
            lax.fori_loop(0, n, drain_pad, 0)

    def issue(t, c):
        row_copy(x3_ref, t, p1_ref[0, 0, t]).start(priority=0)
        row_copy(x3_ref, t, p2_ref[0, 0, t]).start(priority=1)
        return c

    def drain(t, c):
        row_copy(x3_ref, t, p1_ref[0, 0, t]).wait()
        row_copy(x3_ref, t, p2_ref[0, 0, t]).wait()
        return c

    lax.fori_loop(0, tm, issue, 0, unroll=8)
    lax.fori_loop(0, tm, drain, 0, unroll=8)


def _dispatch(x, tabs, S):
    T = x.shape[0]
    tm = _tiles(S)["tm_in"]
    n_rows = tabs["n_tiles"] * MOE_ROWS
    idx_spec = pl.BlockSpec((1, 1, tm), lambda i, pad: (i, 0, 0), memory_space=pltpu.SMEM)
    return pl.pallas_call(
        _dispatch_kernel,
        grid_spec=pltpu.PrefetchScalarGridSpec(
            num_scalar_prefetch=1,
            grid=(T // tm,),
            in_specs=[pl.BlockSpec((tm, D_MODEL), lambda i, pad: (i, 0)), idx_spec, idx_spec],
            out_specs=pl.BlockSpec(memory_space=pl.ANY),
            scratch_shapes=[pltpu.VMEM((tm, SUBLANES, LANES), F32), pltpu.VMEM((1, SUBLANES, LANES), F32),
                            pltpu.SemaphoreType.DMA(())],
        ),
        out_shape=jax.ShapeDtypeStruct((n_rows, SUBLANES, LANES), F32),
        compiler_params=pltpu.CompilerParams(dimension_semantics=("arbitrary",)),
        name="moe_dispatch",
    )(tabs["pad"], x, tabs["pos1"].reshape(T // tm, 1, tm), tabs["pos2"].reshape(T // tm, 1, tm))


def _gmm_kernel(te_ref, nu_ref, x_ref, wg_ref, wu_ref, wd_ref, o_ref, *, tf):
    used = pl.program_id(0) < nu_ref[0]

    @pl.when(jnp.logical_not(used))
    def _():
        o_ref[...] = jnp.zeros_like(o_ref)

    @pl.when(used)
    def _():
        x = x_ref[...].reshape(x_ref.shape[0], D_MODEL)
        y = _swiglu(x.astype(BF16), wg_ref.at[0], wu_ref.at[0], wd_ref.at[0], tf)
        o_ref[...] = y.reshape(o_ref.shape)


def _gmm(xs, tabs, w_gate, w_up, w_down, S):
    R = MOE_ROWS
    tf = _tiles(S)["tf"]
    row_spec = pl.BlockSpec((R, SUBLANES, LANES), lambda j, te, nu: (j, 0, 0))
    w_spec = lambda shape: pl.BlockSpec(shape, lambda j, te, nu: (te[j], 0, 0),
                                        pipeline_mode=pl.Buffered(1))
    return pl.pallas_call(
        functools.partial(_gmm_kernel, tf=tf),
        grid_spec=pltpu.PrefetchScalarGridSpec(
            num_scalar_prefetch=2,
            grid=(tabs["n_tiles"],),
            in_specs=[row_spec, w_spec((1, D_MODEL, D_FF_PAD)), w_spec((1, D_MODEL, D_FF_PAD)),
                      w_spec((1, D_FF_PAD, D_MODEL))],
            out_specs=row_spec,
        ),
        out_shape=jax.ShapeDtypeStruct(xs.shape, F32),
        compiler_params=pltpu.CompilerParams(
            dimension_semantics=("arbitrary",), vmem_limit_bytes=56 * MIB),
        name="moe_gmm",
    )(tabs["tile_expert"], tabs["n_used"], xs, w_gate, w_up, w_down)


def _combine_kernel(x_ref, route_ref, p1_ref, p2_ref, lg_ref, lb_ref, ys_ref, o_ref, y1_ref, y2_ref, sem):
    tm = x_ref.shape[0]

    def row_copies(t):
        return (pltpu.make_async_copy(ys_ref.at[p1_ref[0, 0, t]], y1_ref.at[t], sem),
                pltpu.make_async_copy(ys_ref.at[p2_ref[0, 0, t]], y2_ref.at[t], sem))

    def issue(t, c):
        for priority, cp in enumerate(row_copies(t)):
            cp.start(priority=priority)
        return c

    def drain(t, c):
        for cp in row_copies(t):
            cp.wait()
        return c

    lax.fori_loop(0, tm, issue, 0, unroll=8)
    lax.fori_loop(0, tm, drain, 0, unroll=8)
    route = route_ref[...]
    ff = (route[:, 0:1] * y1_ref[...].reshape(tm, D_MODEL)
          + route[:, 1:2] * y2_ref[...].reshape(tm, D_MODEL))
    o_ref[...] = _layer_norm(ALPHA * x_ref[...] + ff, lg_ref[...], lb_ref[...])


def _combine(x, route, tabs, ys, ln_g, ln_b, S):
    T = x.shape[0]
    tm = _tiles(S)["tm_comb"]
    idx_spec = pl.BlockSpec((1, 1, tm), lambda i: (i, 0, 0), memory_space=pltpu.SMEM)
    return pl.pallas_call(
        _combine_kernel,
        grid=(T // tm,),
        in_specs=[
            pl.BlockSpec((tm, D_MODEL), lambda i: (i, 0)),
            pl.BlockSpec((tm, ROUTER_PAD), lambda i: (i, 0)),
            idx_spec, idx_spec,
            pl.BlockSpec((1, D_MODEL), lambda i: (0, 0)),
            pl.BlockSpec((1, D_MODEL), lambda i: (0, 0)),
            pl.BlockSpec(memory_space=pl.ANY),
        ],
        out_specs=pl.BlockSpec((tm, D_MODEL), lambda i: (i, 0)),
        out_shape=jax.ShapeDtypeStruct((T, D_MODEL), F32),
        scratch_shapes=[pltpu.VMEM((tm, SUBLANES, LANES), F32), pltpu.VMEM((tm, SUBLANES, LANES), F32),
                        pltpu.SemaphoreType.DMA(())],
        compiler_params=pltpu.CompilerParams(dimension_semantics=("arbitrary",)),
        name="moe_combine",
    )(x, route, tabs["pos1"].reshape(T // tm, 1, tm), tabs["pos2"].reshape(T // tm, 1, tm), ln_g, ln_b, ys)


def _rope_tables(S):
    half = DIFF_HEAD_DIM // 2
    inv = ROPE_THETA ** (-jnp.arange(0, DIFF_HEAD_DIM, 2, dtype=F32) / DIFF_HEAD_DIM)
    ang = jnp.arange(S, dtype=F32)[:, None] * inv[None, :]
    cos, sin = jnp.cos(ang), jnp.sin(ang)
    reps = LANES // DIFF_HEAD_DIM
    cos_t = jnp.tile(jnp.concatenate([cos, cos], axis=1), (1, reps))
    sin_t = jnp.tile(jnp.concatenate([-sin, sin], axis=1), (1, reps))
    assert cos_t.shape == (S, LANES) and half * 2 == DIFF_HEAD_DIM
    return cos_t, sin_t


def _dft_tables(S):
    split = math.gcd(S, DFT_ROW_SPLIT)
    k = jnp.arange(S, dtype=jnp.int32)[None, :]

    def thin(rows):
        ang = ((rows[:, None] * k) % S).astype(F32) * (2.0 * math.pi / S)
        return jnp.cos(ang), jnp.sin(ang)

    ch, sh = thin(jnp.arange(S // split, dtype=jnp.int32) * split)
    cl, sl = thin(jnp.arange(split, dtype=jnp.int32))
    scale = S ** -0.5
    cos = (ch[:, None, :] * cl[None, :, :] - sh[:, None, :] * sl[None, :, :]) * scale
    sin = (sh[:, None, :] * cl[None, :, :] + ch[:, None, :] * sl[None, :, :]) * -scale
    return cos.reshape(S, S).astype(BF16), sin.reshape(S, S).astype(BF16)


def _dft64_table():
    j = jnp.arange(FOURIER_GROUP, dtype=jnp.int32)
    ang = ((j[:, None] * j[None, :]) % FOURIER_GROUP).astype(F32) * (2.0 * math.pi / FOURIER_GROUP)
    scale = FOURIER_GROUP ** -0.5
    eye = jnp.eye(N_FOURIER_GROUPS, dtype=F32)
    return jnp.concatenate([jnp.kron(eye, jnp.cos(ang) * scale),
                            jnp.kron(eye, jnp.sin(ang) * scale)], axis=1).astype(BF16)


def _pad_ff(w, axis):
    pad = [(0, 0)] * w.ndim
    pad[axis] = (0, D_FF_PAD - D_FF)
    return jnp.pad(w, pad).astype(BF16)


def _prepare(w_in,w_fourier, w_sgu, w_diff, w_out, vn_g, vn_b, sgu_w, sgu_b,
             lam_q1, lam_k1, lam_q2, lam_k2, subln_g, ln1_g, ln1_b, ln2_g, ln2_b,
             ffn_w_gate, ffn_w_up, ffn_w_down, w_router, moe_w_gate, moe_w_up, moe_w_down):
    w_in_b = w_in.astype(BF16)
    row = lambda a: a.astype(F32)[:, None, :]
    return dict(
        w_a=w_in_b[:, :, :OFF_G], w_g=w_in_b[:, :, OFF_G:],
        w_f=w_fourier.astype(BF16), w_s=w_sgu.astype(BF16), w_d=w_diff.astype(BF16),
        w_o=w_out.astype(BF16),
        vn_g=row(vn_g), vn_b=row(vn_b),
        sgu_wcat=jnp.transpose(sgu_w, (0, 2, 1, 3)).reshape(DEPTH, CHUNK, SGU_HEADS * CHUNK).astype(BF16),
        sgu_bias=jnp.repeat(jnp.transpose(sgu_b, (0, 2, 1)).astype(F32), SGU_HEAD_DIM, axis=2),
        lam=jnp.stack([lam_q1, lam_k1, lam_q2, lam_k2], axis=1).astype(F32),
        subln_g=row(subln_g), ln1_g=row(ln1_g), ln1_b=row(ln1_b), ln2_g=row(ln2_g), ln2_b=row(ln2_b),
        ffn_gate=_pad_ff(ffn_w_gate, 2), ffn_up=_pad_ff(ffn_w_up, 2), ffn_down=_pad_ff(ffn_w_down, 1),
        w_router=jnp.pad(w_router, ((0, 0), (0, 0), (0, ROUTER_PAD - N_EXPERTS))).astype(BF16),
        moe_gate=_pad_ff(moe_w_gate, 3), moe_up=_pad_ff(moe_w_up, 3), moe_down=_pad_ff(moe_w_down, 2),
        dft64=_dft64_table(),
    )


def _trunk(x3, p):
    B, S, _ = x3.shape
    x = x3.reshape(B * S, D_MODEL)
    rope_cos, rope_sin = _rope_tables(S)
    ct, snt = _dft_tables(S)
    for l in range(DEPTH):
        lambda_init = 0.8 - 0.6 * math.exp(-0.3 * l)
        g, so, qt, k, vat = _inproj(x, p["w_a"][l], p["dft64"], p["vn_g"][l], p["vn_b"][l],
                                    p["sgu_wcat"][l], p["sgu_bias"][l], rope_cos, rope_sin, S)
        fo = _fourier(g, ct, snt, B, S)
        do = _attn(p["lam"][l], qt, k, vat, p["subln_g"][l], B, S, lambda_init)
        j = l // 2
        moe = l % 2 == 1
        x1, route, counts = _merge(x, fo, so, do, p["w_g"][l], p["w_f"][l], p["w_s"][l], p["w_d"][l],
                                   p["w_o"][l], p["ln1_g"][l], p["ln1_b"][l],
                                   p["w_router"][j] if moe else None, S)
        if moe:
            tabs = _route_tables(route, counts, B * S)
            xs = _dispatch(x1, tabs, S)
            ys = _gmm(xs, tabs, p["moe_gate"][j], p["moe_up"][j], p["moe_down"][j], S)
            x = _combine(x1, route, tabs, ys, p["ln2_g"][l], p["ln2_b"][l], S)
        else:
            x = _ffn(x1, p["ffn_gate"][j], p["ffn_up"][j], p["ffn_down"][j],
                     p["ln2_g"][l], p["ln2_b"][l], S)
    return x.reshape(B, S, D_MODEL)


def kernel(x_prompt, x_sample, w_in, w_fourier, w_sgu, w_diff, w_out, vn_g, vn_b, sgu_w, sgu_b,
           lam_q1, lam_k1, lam_q2, lam_k2, subln_g, ln1_g, ln1_b, ln2_g, ln2_b,
           ffn_w_gate, ffn_w_up, ffn_w_down, w_router, moe_w_gate, moe_w_up, moe_w_down):
    p = _prepare(w_in, w_fourier, w_sgu, w_diff, w_out, vn_g, vn_b, sgu_w, sgu_b,
                 lam_q1, lam_k1, lam_q2, lam_k2, subln_g, ln1_g, ln1_b, ln2_g, ln2_b,
                 ffn_w_gate, ffn_w_up, ffn_w_down, w_router, moe_w_gate, moe_w_up, moe_w_down)
    return (_trunk(x_prompt, p), _trunk(x_sample, p))
```

```python
import functools
import math

import jax
import jax.numpy as jnp
from jax import lax
from jax.experimental import pallas as pl
from jax.experimental.pallas import tpu as pltpu

F32 = jnp.float32
BF16 = jnp.bfloat16

D_MODEL = 1024
DEPTH = 2
N_FOURIER_GROUPS = 4
FOURIER_GROUP = 64
FOURIER_WIDTH = N_FOURIER_GROUPS * FOURIER_GROUP
SGU_HEADS = 4
SGU_HEAD_DIM = 64
SGU_WIDTH = SGU_HEADS * SGU_HEAD_DIM
CHUNK = 128
DIFF_HEADS = 4
DIFF_HEAD_DIM = 64
DIFF_V_DIM = 2 * DIFF_HEAD_DIM
DIFF_QK_WIDTH = DIFF_HEADS * 2 * DIFF_HEAD_DIM
DIFF_V_WIDTH = DIFF_HEADS * DIFF_V_DIM
ROPE_THETA = 10000.0
N_BRANCHES = 3
D_FF = 2752
N_EXPERTS = 8
ALPHA = (2 * DEPTH) ** 0.25
LN_EPS = 1e-5
RMS_EPS = 1e-5

OFF_F = 0
OFF_U = OFF_F + FOURIER_WIDTH
OFF_V = OFF_U + SGU_WIDTH
OFF_Q = OFF_V + SGU_WIDTH
OFF_K = OFF_Q + DIFF_QK_WIDTH
OFF_VA = OFF_K + DIFF_QK_WIDTH
OFF_G = OFF_VA + DIFF_V_WIDTH
IN_WIDTH = OFF_G + N_BRANCHES * D_MODEL

TOP_K = 2
LANES = 128
SUBLANES = 8
MXU_WIDTH = 256
MOE_ROWS = 512
DFT_ROW_SPLIT = 64
SCORE_LEAD = 2
D_FF_PAD = 2816
ROUTER_PAD = LANES
MIB = 1024 * 1024


def _tiles(S):
    t = dict(
        tm_in=min(512, S),
        tm_comb=min(256, S),
        tm_dft=min(512, S),
        tq=min(512, S),
        tkc=min(128, S),
        tkp=min(512, S),
        tf=D_FF_PAD // 2,
    )
    return t


def _layer_norm(z, g, b):
    mu = jnp.mean(z, axis=-1, keepdims=True)
    zc = z - mu
    var = jnp.mean(zc * zc, axis=-1, keepdims=True)
    return zc * lax.rsqrt(var + LN_EPS) * g + b


def _dot(a, b):
    return jnp.dot(a, b, preferred_element_type=F32)


def _inproj_kernel(x_ref, w_ref, dft_ref, vng_ref, vnb_ref, sguw_ref, sgub_ref, cos_ref, sin_ref,
                   g_out, so_out, qt_out, k_out, vat_out):
    tm = x_ref.shape[0]
    xb = x_ref[...].astype(BF16)

    def proj(a, b):
        return _dot(xb, w_ref[:, a:b])

    f = proj(OFF_F, OFF_U)
    g_out[...] = _dot(f.astype(BF16), dft_ref[...]).astype(BF16)

    u = proj(OFF_U, OFF_V)
    v = proj(OFF_V, OFF_Q)
    vn = _layer_norm(v, vng_ref[...], vnb_ref[...]).astype(BF16)
    row_head = lax.broadcasted_iota(jnp.int32, (SGU_HEADS * CHUNK, SGU_WIDTH), 0) // CHUNK
    col_head = lax.broadcasted_iota(jnp.int32, (SGU_HEADS * CHUNK, SGU_WIDTH), 1) // SGU_HEAD_DIM
    head_mask = row_head == col_head
    for c in range(tm // CHUNK):
        rows = slice(c * CHUNK, (c + 1) * CHUNK)
        vb = vn[rows, :]
        rhs = jnp.where(head_mask, jnp.concatenate([vb] * SGU_HEADS, axis=0), jnp.zeros((), BF16))
        mixed = _dot(sguw_ref[...], rhs) + sgub_ref[...]
        so_out[rows, :] = (u[rows, :] * mixed).astype(BF16)

    cs = cos_ref[...]
    sn = sin_ref[...]
    lane = lax.broadcasted_iota(jnp.int32, (tm, LANES), 1)
    first_half = (lane % DIFF_HEAD_DIM) < (DIFF_HEAD_DIM // 2)

    def rope(t):
        outs = []
        for j in range(DIFF_QK_WIDTH // LANES):
            tc = t[:, j * LANES:(j + 1) * LANES]
            rot = jnp.where(first_half,
                            pltpu.roll(tc, LANES - DIFF_HEAD_DIM // 2, 1),
                            pltpu.roll(tc, DIFF_HEAD_DIM // 2, 1))
            outs.append(tc * cs + rot * sn)
        return jnp.concatenate(outs, axis=1)

    q = rope(proj(OFF_Q, OFF_K)) * (DIFF_HEAD_DIM ** -0.5)
    qt_out[...] = q.T.astype(BF16)
    k_out[...] = rope(proj(OFF_K, OFF_VA)).astype(BF16)
    vat_out[...] = proj(OFF_VA, OFF_G).T.astype(BF16)


def _inproj(x, w_a, dft64, vn_g, vn_b, sgu_wcat, sgu_bias, rope_cos, rope_sin, S):
    T = x.shape[0]
    tm = _tiles(S)["tm_in"]
    n_pos = S // tm
    full = lambda shape: pl.BlockSpec(shape, lambda i: (0,) * len(shape))
    return pl.pallas_call(
        _inproj_kernel,
        grid=(T // tm,),
        in_specs=[
            pl.BlockSpec((tm, D_MODEL), lambda i: (i, 0)),
            full((D_MODEL, OFF_G)),
            full((FOURIER_WIDTH, 2 * FOURIER_WIDTH)),
            full((1, SGU_WIDTH)),
            full((1, SGU_WIDTH)),
            full((CHUNK, SGU_HEADS * CHUNK)),
            full((CHUNK, SGU_WIDTH)),
            pl.BlockSpec((tm, LANES), lambda i: (i % n_pos, 0)),
            pl.BlockSpec((tm, LANES), lambda i: (i % n_pos, 0)),
        ],
        out_specs=[
            pl.BlockSpec((tm, 2 * FOURIER_WIDTH), lambda i: (i, 0)),
            pl.BlockSpec((tm, SGU_WIDTH), lambda i: (i, 0)),
            pl.BlockSpec((DIFF_QK_WIDTH, tm), lambda i: (0, i)),
            pl.BlockSpec((tm, DIFF_QK_WIDTH), lambda i: (i, 0)),
            pl.BlockSpec((DIFF_V_WIDTH, tm), lambda i: (0, i)),
        ],
        out_shape=[
            jax.ShapeDtypeStruct((T, 2 * FOURIER_WIDTH), BF16),
            jax.ShapeDtypeStruct((T, SGU_WIDTH), BF16),
            jax.ShapeDtypeStruct((DIFF_QK_WIDTH, T), BF16),
            jax.ShapeDtypeStruct((T, DIFF_QK_WIDTH), BF16),
            jax.ShapeDtypeStruct((DIFF_V_WIDTH, T), BF16),
        ],
        compiler_params=pltpu.CompilerParams(
            dimension_semantics=("parallel",), vmem_limit_bytes=48 * MIB),
        name="inproj",
    )(x, w_a, dft64, vn_g, vn_b, sgu_wcat, sgu_bias, rope_cos, rope_sin)


def _fourier_kernel(ct_ref, snt_ref, g_ref, o_ref):
    acc = _dot(ct_ref[...], g_ref[:, :FOURIER_WIDTH])
    acc = acc + _dot(snt_ref[...], g_ref[:, FOURIER_WIDTH:])
    o_ref[...] = acc.astype(BF16)


def _fourier(g, ct, snt, B, S):
    tm = _tiles(S)["tm_dft"]
    n_row = S // tm
    return pl.pallas_call(
        _fourier_kernel,
        grid=(n_row, B),
        in_specs=[
            pl.BlockSpec((tm, S), lambda i, b: (i, 0), pipeline_mode=pl.Buffered(1)),
            pl.BlockSpec((tm, S), lambda i, b: (i, 0), pipeline_mode=pl.Buffered(1)),
            pl.BlockSpec((S, 2 * FOURIER_WIDTH), lambda i, b: (b, 0)),
        ],
        out_specs=pl.BlockSpec((tm, FOURIER_WIDTH), lambda i, b: (b * n_row + i, 0)),
        out_shape=jax.ShapeDtypeStruct((B * S, FOURIER_WIDTH), BF16),
        compiler_params=pltpu.CompilerParams(
            dimension_semantics=("parallel", "parallel"), vmem_limit_bytes=48 * MIB),
        name="fourier",
    )(ct, snt, g)


def _attn_kernel(lam_ref, qt_ref, k_ref, vt_ref, g_ref, o_ref, s_ref, e_ref, m_ref, *, tkp, lambda_init):
    tq = qt_ref.shape[1]
    width = 2 * tq
    S = k_ref.shape[0]
    n_p = S // tkp
    groups = tkp // SUBLANES
    ring = s_ref.shape[0]
    lp = lam_ref[...]
    lam = (jnp.exp(jnp.sum(lp[0:1, :] * lp[1:2, :], axis=-1, keepdims=True))
           - jnp.exp(jnp.sum(lp[2:3, :] * lp[3:4, :], axis=-1, keepdims=True))
           + lambda_init)
    qt = qt_ref[...]
    row = lax.broadcasted_iota(jnp.int32, qt.shape, 0)
    zero = jnp.zeros((), BF16)
    qt_both = jnp.concatenate(
        [jnp.where(row < DIFF_HEAD_DIM, qt, zero), jnp.where(row >= DIFF_HEAD_DIM, qt, zero)], axis=1)

    m_class = jnp.full((SUBLANES, width), -jnp.inf, F32)
    m_exp = m_class
    l_class = jnp.zeros((SUBLANES, width), F32)
    l_hist = []
    for p in range(n_p + 1):
        if p < n_p:
            hold = 0
            if p >= ring:
                bits = pltpu.bitcast(l_hist[p - ring], jnp.int32)[0, 0]
                half = jnp.int32(16)
                hold = lax.shift_right_logical(lax.shift_right_logical(bits, half), half)
            kc = k_ref[pl.ds(pl.multiple_of(p * tkp + hold, tkp), tkp), :]
            s = _dot(kc, qt_both)
            s_ref[p % ring] = s
            m_class = jnp.maximum(m_class, jnp.max(s.reshape(groups, SUBLANES, width), axis=0))
            m_ref[p] = m_class
        if p >= 1:
            j = p - 1
            m_new = m_ref[j]
            e = jnp.exp(s_ref[j % ring].reshape(groups, SUBLANES, width) - m_new[None])
            e_ref[j * tkp:(j + 1) * tkp, :] = e.reshape(tkp, width).astype(BF16)
            l_class = l_class * jnp.exp(m_exp - m_new) + jnp.sum(e, axis=0)
            m_exp = m_new
            l_hist.append(l_class)

    m_all = jnp.max(m_exp, axis=0, keepdims=True)
    l_all = jnp.sum(l_class * jnp.exp(m_exp - m_all), axis=0, keepdims=True)
    scale = jnp.concatenate([1.0 / l_all[:, :tq], lam / l_all[:, tq:]], axis=1)
    packed = 2 * SUBLANES
    acc = jnp.zeros((DIFF_V_DIM, tq), F32)
    for p in range(n_p):
        f = jnp.exp(m_ref[p] - m_all) * scale
        f = jnp.concatenate([f, f], axis=0).astype(BF16)
        ew = e_ref[p * tkp:(p + 1) * tkp, :].reshape(tkp // packed, packed, width) * f[None]
        a = (ew[:, :, :tq] - ew[:, :, tq:]).reshape(tkp, tq)
        acc = acc + _dot(vt_ref[:, p * tkp:(p + 1) * tkp], a)

    o = acc.T
    o = o * lax.rsqrt(jnp.mean(o * o, axis=-1, keepdims=True) + RMS_EPS) * g_ref[...]
    o_ref[...] = (o * (1.0 - lambda_init)).astype(BF16)


def _attn(lam_params, qt, k, vat, subln_g, B, S, lambda_init):
    t = _tiles(S)
    tq, tkp = t["tq"], t["tkp"]
    n_q = S // tq
    kern = functools.partial(_attn_kernel, tkp=tkp, lambda_init=lambda_init)
    return pl.pallas_call(
        kern,
        grid=(B, DIFF_HEADS, n_q),
        in_specs=[
            pl.BlockSpec((4, DIFF_HEAD_DIM), lambda b, h, i: (0, 0)),
            pl.BlockSpec((DIFF_V_DIM, tq), lambda b, h, i: (h, b * n_q + i)),
            pl.BlockSpec((S, DIFF_V_DIM), lambda b, h, i: (b, h)),
            pl.BlockSpec((DIFF_V_DIM, S), lambda b, h, i: (h, b)),
            pl.BlockSpec((1, DIFF_V_DIM), lambda b, h, i: (0, 0)),
        ],
        out_specs=pl.BlockSpec((tq, DIFF_V_DIM), lambda b, h, i: (b * n_q + i, h)),
        out_shape=jax.ShapeDtypeStruct((B * S, DIFF_V_WIDTH), BF16),
        scratch_shapes=[pltpu.VMEM((min(SCORE_LEAD, S // tkp), tkp, 2 * tq), F32),
                        pltpu.VMEM((S, 2 * tq), BF16),
                        pltpu.VMEM((S // tkp, SUBLANES, 2 * tq), F32)],
        compiler_params=pltpu.CompilerParams(
            dimension_semantics=("parallel", "parallel", "parallel"), vmem_limit_bytes=48 * MIB),
        name="attn",
    )(lam_params, qt, k, vat, subln_g)


def _merge_kernel(*refs, with_router):
    if with_router:
        (x_ref, fo_ref, so_ref, do_ref, wg_ref, wf_ref, ws_ref, wd_ref, wo_ref, lg_ref, lb_ref,
         wr_ref, x1_out, route_out, counts_out, base_ref) = refs
    else:
        (x_ref, fo_ref, so_ref, do_ref, wg_ref, wf_ref, ws_ref, wd_ref, wo_ref, lg_ref, lb_ref,
         x1_out) = refs
    x = x_ref[...]
    xb = x.astype(BF16)
    merged = None
    for j, (br_ref, wb_ref) in enumerate(((fo_ref, wf_ref), (so_ref, ws_ref), (do_ref, wd_ref))):
        gate = jax.nn.sigmoid(_dot(xb, wg_ref[:, j * D_MODEL:(j + 1) * D_MODEL]))
        term = gate * _dot(br_ref[...], wb_ref[...])
        merged = term if merged is None else merged + term
    y = _dot(merged.astype(BF16), wo_ref[...])
    x1 = _layer_norm(ALPHA * x + y, lg_ref[...], lb_ref[...])
    x1_out[...] = x1

    if with_router:
        tm = x.shape[0]
        logits = _dot(x1.astype(BF16), wr_ref[...])
        lane = lax.broadcasted_iota(jnp.int32, (tm, ROUTER_PAD), 1)
        neg = jnp.float32(-jnp.inf)
        lg = jnp.where(lane < N_EXPERTS, logits, neg)
        v1 = jnp.max(lg, axis=-1, keepdims=True)
        i1 = jnp.min(jnp.where(lg == v1, lane, ROUTER_PAD), axis=-1, keepdims=True)
        lg2 = jnp.where(lane == i1, neg, lg)
        v2 = jnp.max(lg2, axis=-1, keepdims=True)
        i2 = jnp.min(jnp.where(lg2 == v2, lane, ROUTER_PAD), axis=-1, keepdims=True)
        e2 = jnp.exp(v2 - v1)
        den = 1.0 + e2

        @pl.when(pl.program_id(0) == 0)
        def _():
            base_ref[...] = jnp.zeros_like(base_ref)

        pick1 = lane == i1
        pick2 = lane == i2
        sel = jnp.logical_or(pick1, pick2)
        r_id = lax.broadcasted_iota(jnp.int32, (tm, tm), 0)
        c_id = lax.broadcasted_iota(jnp.int32, (tm, tm), 1)
        ltri = (c_id < r_id).astype(BF16)
        base = base_ref[0:1, :]
        rank = base + _dot(ltri, sel.astype(BF16))
        r1 = jnp.sum(jnp.where(pick1, rank, 0.0), axis=-1, keepdims=True)
        r2 = jnp.sum(jnp.where(pick2, rank, 0.0), axis=-1, keepdims=True)
        new_base = base + jnp.sum(sel.astype(F32), axis=0, keepdims=True)
        base_ref[...] = jnp.broadcast_to(new_base, base_ref.shape)
        counts_out[...] = jnp.broadcast_to(new_base, counts_out.shape)
        fields = (1.0 / den, e2 / den, i1.astype(F32), i2.astype(F32), r1, r2)
        route = jnp.zeros((tm, ROUTER_PAD), F32)
        for n, val in enumerate(fields):
            route = jnp.where(lane == n, val, route)
        route_out[...] = route


def _merge(x, fo, so, do, w_g, w_f, w_s, w_d, w_o, ln_g, ln_b, w_router, S):
    T = x.shape[0]
    tm = _tiles(S)["tm_in"]
    with_router = w_router is not None
    full = lambda shape: pl.BlockSpec(shape, lambda i: (0,) * len(shape))
    row = lambda width: pl.BlockSpec((tm, width), lambda i: (i, 0))
    in_specs = [
        row(D_MODEL), row(FOURIER_WIDTH), row(SGU_WIDTH), row(DIFF_V_WIDTH),
        full((D_MODEL, N_BRANCHES * D_MODEL)),
        full((FOURIER_WIDTH, D_MODEL)), full((SGU_WIDTH, D_MODEL)), full((DIFF_V_WIDTH, D_MODEL)),
        full((D_MODEL, D_MODEL)), full((1, D_MODEL)), full((1, D_MODEL)),
    ]
    args = [x, fo, so, do, w_g, w_f, w_s, w_d, w_o, ln_g, ln_b]
    out_specs = [row(D_MODEL)]
    out_shape = [jax.ShapeDtypeStruct((T, D_MODEL), F32)]
    scratch = []
    if with_router:
        in_specs.append(full((D_MODEL, ROUTER_PAD)))
        args.append(w_router)
        out_specs += [row(ROUTER_PAD), full((SUBLANES, ROUTER_PAD))]
        out_shape += [jax.ShapeDtypeStruct((T, ROUTER_PAD), F32),
                      jax.ShapeDtypeStruct((SUBLANES, ROUTER_PAD), F32)]
        scratch.append(pltpu.VMEM((SUBLANES, ROUTER_PAD), F32))
    outs = pl.pallas_call(
        functools.partial(_merge_kernel, with_router=with_router),
        grid=(T // tm,),
        in_specs=in_specs,
        out_specs=out_specs,
        out_shape=out_shape,
        scratch_shapes=scratch,
        compiler_params=pltpu.CompilerParams(
            dimension_semantics=("arbitrary" if with_router else "parallel",),
            vmem_limit_bytes=56 * MIB),
        name="merge_router" if with_router else "merge",
    )(*args)
    return (outs[0], outs[1], outs[2]) if with_router else (outs[0], None, None)


def _swiglu(xb, wg_ref, wu_ref, wd_ref, tf):
    y = None
    for f in range(D_FF_PAD // tf):
        cols = slice(f * tf, (f + 1) * tf)
        h = jax.nn.silu(_dot(xb, wg_ref[:, cols])) * _dot(xb, wu_ref[:, cols])
        part = _dot(h.astype(BF16), wd_ref[cols, :])
        y = part if y is None else y + part
    return y


def _ffn_kernel(x_ref, wg_ref, wu_ref, wd_ref, lg_ref, lb_ref, o_ref, *, tf):
    x = x_ref[...]
    ff = _swiglu(x.astype(BF16), wg_ref, wu_ref, wd_ref, tf)
    o_ref[...] = _layer_norm(ALPHA * x + ff, lg_ref[...], lb_ref[...])


def _ffn(x, w_gate, w_up, w_down, ln_g, ln_b, S):
    T = x.shape[0]
    t = _tiles(S)
    tm, tf = t["tm_in"], t["tf"]
    resident = lambda shape: pl.BlockSpec(shape, lambda i: (0, 0), pipeline_mode=pl.Buffered(1))
    return pl.pallas_call(
        functools.partial(_ffn_kernel, tf=tf),
        grid=(T // tm,),
        in_specs=[
            pl.BlockSpec((tm, D_MODEL), lambda i: (i, 0)),
            resident((D_MODEL, D_FF_PAD)), resident((D_MODEL, D_FF_PAD)), resident((D_FF_PAD, D_MODEL)),
            pl.BlockSpec((1, D_MODEL), lambda i: (0, 0)),
            pl.BlockSpec((1, D_MODEL), lambda i: (0, 0)),
        ],
        out_specs=pl.BlockSpec((tm, D_MODEL), lambda i: (i, 0)),
        out_shape=jax.ShapeDtypeStruct((T, D_MODEL), F32),
        compiler_params=pltpu.CompilerParams(
            dimension_semantics=("parallel",), vmem_limit_bytes=56 * MIB),
        name="ffn_dense",
    )(x, w_gate, w_up, w_down, ln_g, ln_b)


def _route_tables(route, counts, T):
    R = MOE_ROWS
    as_int = lambda n: route[:, n].astype(jnp.int32)
    i1, i2, r1, r2 = as_int(2), as_int(3), as_int(4), as_int(5)
    cnt = counts[0, :N_EXPERTS].astype(jnp.int32)
    padded = (cnt + R - 1) // R * R
    ends = jnp.cumsum(padded)
    off = ends - padded
    n_tiles = TOP_K * T // R + N_EXPERTS
    tile_expert = jnp.minimum(
        jnp.searchsorted(ends, jnp.arange(n_tiles, dtype=jnp.int32) * R, side="right"),
        N_EXPERTS - 1).astype(jnp.int32)
    pad_start = off + cnt
    pad_n = (padded - cnt).at[N_EXPERTS - 1].set(n_tiles * R - pad_start[N_EXPERTS - 1])
    return dict(
        pos1=jnp.take(off, i1) + r1, pos2=jnp.take(off, i2) + r2,
        tile_expert=tile_expert, n_used=(ends[-1:] // R).astype(jnp.int32),
        pad=jnp.concatenate([pad_start, pad_n]).astype(jnp.int32), n_tiles=n_tiles)


def _dispatch_kernel(pad_ref, x_ref, p1_ref, p2_ref, xs_ref, x3_ref, zero_ref, sem):
    tm = x_ref.shape[0]
    x3_ref[...] = x_ref[...].reshape(tm, SUBLANES, LANES)

    def row_copy(src_ref, src_row, dst_row):
        return pltpu.make_async_copy(src_ref.at[src_row], xs_ref.at[dst_row], sem)

    @pl.when(pl.program_id(0) == 0)
    def _():
        zero_ref[...] = jnp.zeros_like(zero_ref)
        for e in range(N_EXPERTS):
            start = pad_ref[e]
            n = pad_ref[N_EXPERTS + e]

            def issue_pad(k, c):
                row_copy(zero_ref, 0, start + k).start()
                return c

            def drain_pad(k, c):
                row_copy(zero_ref, 0, start + k).wait()
                return c

            lax.fori_loop(0, n, issue_pad, 0)
            lax.fori_loop(0, n, drain_pad, 0)

    def issue(t, c):
        row_copy(x3_ref, t, p1_ref[0, 0, t]).start(priority=0)
        row_copy(x3_ref, t, p2_ref[0, 0, t]).start(priority=1)
        return c

    def drain(t, c):
        row_copy(x3_ref, t, p1_ref[0, 0, t]).wait()
        row_copy(x3_ref, t, p2_ref[0, 0, t]).wait()
        return c

    lax.fori_loop(0, tm, issue, 0, unroll=8)
    lax.fori_loop(0, tm, drain, 0, unroll=8)


def _dispatch(x, tabs, S):
    T = x.shape[0]
    tm = _tiles(S)["tm_in"]
    n_rows = tabs["n_tiles"] * MOE_ROWS
    idx_spec = pl.BlockSpec((1, 1, tm), lambda i, pad: (i, 0, 0), memory_space=pltpu.SMEM)
    return pl.pallas_call(
        _dispatch_kernel,
        grid_spec=pltpu.PrefetchScalarGridSpec(
            num_scalar_prefetch=1,
            grid=(T // tm,),
            in_specs=[pl.BlockSpec((tm, D_MODEL), lambda i, pad: (i, 0)), idx_spec, idx_spec],
            out_specs=pl.BlockSpec(memory_space=pl.ANY),
            scratch_shapes=[pltpu.VMEM((tm, SUBLANES, LANES), F32), pltpu.VMEM((1, SUBLANES, LANES), F32),
                            pltpu.SemaphoreType.DMA(())],
        ),
        out_shape=jax.ShapeDtypeStruct((n_rows, SUBLANES, LANES), F32),
        compiler_params=pltpu.CompilerParams(dimension_semantics=("arbitrary",)),
        name="moe_dispatch",
    )(tabs["pad"], x, tabs["pos1"].reshape(T // tm, 1, tm), tabs["pos2"].reshape(T // tm, 1, tm))


def _gmm_kernel(te_ref, nu_ref, x_ref, wg_ref, wu_ref, wd_ref, o_ref, *, tf):
    used = pl.program_id(0) < nu_ref[0]

    @pl.when(jnp.logical_not(used))
    def _():
        o_ref[...] = jnp.zeros_like(o_ref)

    @pl.when(used)
    def _():
        x = x_ref[...].reshape(x_ref.shape[0], D_MODEL)
        y = _swiglu(x.astype(BF16), wg_ref.at[0], wu_ref.at[0], wd_ref.at[0], tf)
        o_ref[...] = y.reshape(o_ref.shape)


def _gmm(xs, tabs, w_gate, w_up, w_down, S):
    R = MOE_ROWS
    tf = _tiles(S)["tf"]
    row_spec = pl.BlockSpec((R, SUBLANES, LANES), lambda j, te, nu: (j, 0, 0))
    w_spec = lambda shape: pl.BlockSpec(shape, lambda j, te, nu: (te[j], 0, 0),
                                        pipeline_mode=pl.Buffered(1))
    return pl.pallas_call(
        functools.partial(_gmm_kernel, tf=tf),
        grid_spec=pltpu.PrefetchScalarGridSpec(
            num_scalar_prefetch=2,
            grid=(tabs["n_tiles"],),
            in_specs=[row_spec, w_spec((1, D_MODEL, D_FF_PAD)), w_spec((1, D_MODEL, D_FF_PAD)),
                      w_spec((1, D_FF_PAD, D_MODEL))],
            out_specs=row_spec,
        ),
        out_shape=jax.ShapeDtypeStruct(xs.shape, F32),
        compiler_params=pltpu.CompilerParams(
            dimension_semantics=("arbitrary",), vmem_limit_bytes=56 * MIB),
        name="moe_gmm",
    )(tabs["tile_expert"], tabs["n_used"], xs, w_gate, w_up, w_down)


def _combine_kernel(x_ref, route_ref, p1_ref, p2_ref, n1_ref, n2_ref, lg_ref, lb_ref, ys_ref, o_ref,
                    y1_ref, y2_ref, sems):
    tm = x_ref.shape[0]
    step = pl.program_id(0)
    n_steps = pl.num_programs(0)

    def row_copies(pa_ref, pb_ref, slot, t):
        return (pltpu.make_async_copy(ys_ref.at[pa_ref[0, 0, t]], y1_ref.at[slot, t], sems.at[slot]),
                pltpu.make_async_copy(ys_ref.at[pb_ref[0, 0, t]], y2_ref.at[slot, t], sems.at[slot]))

    def issue_tile(pa_ref, pb_ref, slot):
        def issue(t, c):
            for priority, cp in enumerate(row_copies(pa_ref, pb_ref, slot, t)):
                cp.start(priority=priority)
            return c
        lax.fori_loop(0, tm, issue, 0, unroll=8)

    def drain_tile(pa_ref, pb_ref, slot):
        def drain(t, c):
            for cp in row_copies(pa_ref, pb_ref, slot, t):
                cp.wait()
            return c
        lax.fori_loop(0, tm, drain, 0, unroll=8)

    @pl.when(step == 0)
    def _():
        issue_tile(p1_ref, p2_ref, 0)

    for slot in range(2):
        @pl.when(step % 2 == slot)
        def _():
            @pl.when(step + 1 < n_steps)
            def _():
                issue_tile(n1_ref, n2_ref, 1 - slot)

            drain_tile(p1_ref, p2_ref, slot)
            route = route_ref[...]
            ff = (route[:, 0:1] * y1_ref[slot].reshape(tm, D_MODEL)
                  + route[:, 1:2] * y2_ref[slot].reshape(tm, D_MODEL))
            o_ref[...] = _layer_norm(ALPHA * x_ref[...] + ff, lg_ref[...], lb_ref[...])


def _combine(x, route, tabs, ys, ln_g, ln_b, S):
    T = x.shape[0]
    tm = _tiles(S)["tm_comb"]
    n_steps = T // tm
    idx_spec = pl.BlockSpec((1, 1, tm), lambda i: (i, 0, 0), memory_space=pltpu.SMEM)
    next_spec = pl.BlockSpec((1, 1, tm), lambda i: (jnp.minimum(i + 1, n_steps - 1), 0, 0),
                             memory_space=pltpu.SMEM)
    pos1 = tabs["pos1"].reshape(n_steps, 1, tm)
    pos2 = tabs["pos2"].reshape(n_steps, 1, tm)
    return pl.pallas_call(
        _combine_kernel,
        grid=(n_steps,),
        in_specs=[
            pl.BlockSpec((tm, D_MODEL), lambda i: (i, 0)),
            pl.BlockSpec((tm, ROUTER_PAD), lambda i: (i, 0)),
            idx_spec, idx_spec, next_spec, next_spec,
            pl.BlockSpec((1, D_MODEL), lambda i: (0, 0)),
            pl.BlockSpec((1, D_MODEL), lambda i: (0, 0)),
            pl.BlockSpec(memory_space=pl.ANY),
        ],
        out_specs=pl.BlockSpec((tm, D_MODEL), lambda i: (i, 0)),
        out_shape=jax.ShapeDtypeStruct((T, D_MODEL), F32),
        scratch_shapes=[pltpu.VMEM((2, tm, SUBLANES, LANES), F32), pltpu.VMEM((2, tm, SUBLANES, LANES), F32),
                        pltpu.SemaphoreType.DMA((2,))],
        compiler_params=pltpu.CompilerParams(dimension_semantics=("arbitrary",)),
        name="moe_combine",
    )(x, route, pos1, pos2, pos1, pos2, ln_g, ln_b, ys)


def _rope_tables(S):
    half = DIFF_HEAD_DIM // 2
    inv = ROPE_THETA ** (-jnp.arange(0, DIFF_HEAD_DIM, 2, dtype=F32) / DIFF_HEAD_DIM)
    ang = jnp.arange(S, dtype=F32)[:, None] * inv[None, :]
    cos, sin = jnp.cos(ang), jnp.sin(ang)
    reps = LANES // DIFF_HEAD_DIM
    cos_t = jnp.tile(jnp.concatenate([cos, cos], axis=1), (1, reps))
    sin_t = jnp.tile(jnp.concatenate([-sin, sin], axis=1), (1, reps))
    assert cos_t.shape == (S, LANES) and half * 2 == DIFF_HEAD_DIM
    return cos_t, sin_t


def _dft_tables(S):
    split = math.gcd(S, DFT_ROW_SPLIT)
    k = jnp.arange(S, dtype=jnp.int32)[None, :]

    def thin(rows):
        ang = ((rows[:, None] * k) % S).astype(F32) * (2.0 * math.pi / S)
        return jnp.cos(ang), jnp.sin(ang)

    ch, sh = thin(jnp.arange(S // split, dtype=jnp.int32) * split)
    cl, sl = thin(jnp.arange(split, dtype=jnp.int32))
    scale = S ** -0.5
    cos = (ch[:, None, :] * cl[None, :, :] - sh[:, None, :] * sl[None, :, :]) * scale
    sin = (sh[:, None, :] * cl[None, :, :] + ch[:, None, :] * sl[None, :, :]) * -scale
    return cos.reshape(S, S).astype(BF16), sin.reshape(S, S).astype(BF16)


def _dft64_table():
    j = jnp.arange(FOURIER_GROUP, dtype=jnp.int32)
    ang = ((j[:, None] * j[None, :]) % FOURIER_GROUP).astype(F32) * (2.0 * math.pi / FOURIER_GROUP)
    scale = FOURIER_GROUP ** -0.5
    eye = jnp.eye(N_FOURIER_GROUPS, dtype=F32)
    return jnp.concatenate([jnp.kron(eye, jnp.cos(ang) * scale),
                            jnp.kron(eye, jnp.sin(ang) * scale)], axis=1).astype(BF16)


def _pad_ff(w, axis):
    pad = [(0, 0)] * w.ndim
    pad[axis] = (0, D_FF_PAD - D_FF)
    return jnp.pad(w, pad).astype(BF16)


def _prepare(w_in,w_fourier, w_sgu, w_diff, w_out, vn_g, vn_b, sgu_w, sgu_b,
             lam_q1, lam_k1, lam_q2, lam_k2, subln_g, ln1_g, ln1_b, ln2_g, ln2_b,
             ffn_w_gate, ffn_w_up, ffn_w_down, w_router, moe_w_gate, moe_w_up, moe_w_down):
    w_in_b = w_in.astype(BF16)
    row = lambda a: a.astype(F32)[:, None, :]
    return dict(
        w_a=w_in_b[:, :, :OFF_G], w_g=w_in_b[:, :, OFF_G:],
        w_f=w_fourier.astype(BF16), w_s=w_sgu.astype(BF16), w_d=w_diff.astype(BF16),
        w_o=w_out.astype(BF16),
        vn_g=row(vn_g), vn_b=row(vn_b),
        sgu_wcat=jnp.transpose(sgu_w, (0, 2, 1, 3)).reshape(DEPTH, CHUNK, SGU_HEADS * CHUNK).astype(BF16),
        sgu_bias=jnp.repeat(jnp.transpose(sgu_b, (0, 2, 1)).astype(F32), SGU_HEAD_DIM, axis=2),
        lam=jnp.stack([lam_q1, lam_k1, lam_q2, lam_k2], axis=1).astype(F32),
        subln_g=row(subln_g), ln1_g=row(ln1_g), ln1_b=row(ln1_b), ln2_g=row(ln2_g), ln2_b=row(ln2_b),
        ffn_gate=_pad_ff(ffn_w_gate, 2), ffn_up=_pad_ff(ffn_w_up, 2), ffn_down=_pad_ff(ffn_w_down, 1),
        w_router=jnp.pad(w_router, ((0, 0), (0, 0), (0, ROUTER_PAD - N_EXPERTS))).astype(BF16),
        moe_gate=_pad_ff(moe_w_gate, 3), moe_up=_pad_ff(moe_w_up, 3), moe_down=_pad_ff(moe_w_down, 2),
        dft64=_dft64_table(),
    )


def _trunk(x3, p):
    B, S, _ = x3.shape
    x = x3.reshape(B * S, D_MODEL)
    rope_cos, rope_sin = _rope_tables(S)
    ct, snt = _dft_tables(S)
    for l in range(DEPTH):
        lambda_init = 0.8 - 0.6 * math.exp(-0.3 * l)
        g, so, qt, k, vat = _inproj(x, p["w_a"][l], p["dft64"], p["vn_g"][l], p["vn_b"][l],
                                    p["sgu_wcat"][l], p["sgu_bias"][l], rope_cos, rope_sin, S)
        fo = _fourier(g, ct, snt, B, S)
        do = _attn(p["lam"][l], qt, k, vat, p["subln_g"][l], B, S, lambda_init)
        j = l // 2
        moe = l % 2 == 1
        x1, route, counts = _merge(x, fo, so, do, p["w_g"][l], p["w_f"][l], p["w_s"][l], p["w_d"][l],
                                   p["w_o"][l], p["ln1_g"][l], p["ln1_b"][l],
                                   p["w_router"][j] if moe else None, S)
        if moe:
            tabs = _route_tables(route, counts, B * S)
            xs = _dispatch(x1, tabs, S)
            ys = _gmm(xs, tabs, p["moe_gate"][j], p["moe_up"][j], p["moe_down"][j], S)
            x = _combine(x1, route, tabs, ys, p["ln2_g"][l], p["ln2_b"][l], S)
        else:
            x = _ffn(x1, p["ffn_gate"][j], p["ffn_up"][j], p["ffn_down"][j],
                     p["ln2_g"][l], p["ln2_b"][l], S)
    return x.reshape(B, S, D_MODEL)


def kernel(x_prompt, x_sample, w_in, w_fourier, w_sgu, w_diff, w_out, vn_g, vn_b, sgu_w, sgu_b,
           lam_q1, lam_k1, lam_q2, lam_k2, subln_g, ln1_g, ln1_b, ln2_g, ln2_b,
           ffn_w_gate, ffn_w_up, ffn_w_down, w_router, moe_w_gate, moe_w_up, moe_w_down):
    p = _prepare(w_in, w_fourier, w_sgu, w_diff, w_out, vn_g, vn_b, sgu_w, sgu_b,
                 lam_q1, lam_k1, lam_q2, lam_k2, subln_g, ln1_g, ln1_b, ln2_g, ln2_b,
                 ffn_w_gate, ffn_w_up, ffn_w_down, w_router, moe_w_gate, moe_w_up, moe_w_down)
    return (_trunk(x_prompt, p), _trunk(x_sample, p))
```

```python
import functools
import math

import jax
import jax.numpy as jnp
from jax import lax
from jax.experimental import pallas as pl
from jax.experimental.pallas import tpu as pltpu

F32 = jnp.float32
BF16 = jnp.bfloat16

D_MODEL = 1024
DEPTH = 2
N_FOURIER_GROUPS = 4
FOURIER_GROUP = 64
FOURIER_WIDTH = N_FOURIER_GROUPS * FOURIER_GROUP
SGU_HEADS = 4
SGU_HEAD_DIM = 64
SGU_WIDTH = SGU_HEADS * SGU_HEAD_DIM
CHUNK = 128
DIFF_HEADS = 4
DIFF_HEAD_DIM = 64
DIFF_V_DIM = 2 * DIFF_HEAD_DIM
DIFF_QK_WIDTH = DIFF_HEADS * 2 * DIFF_HEAD_DIM
DIFF_V_WIDTH = DIFF_HEADS * DIFF_V_DIM
ROPE_THETA = 10000.0
N_BRANCHES = 3
D_FF = 2752
N_EXPERTS = 8
ALPHA = (2 * DEPTH) ** 0.25
LN_EPS = 1e-5
RMS_EPS = 1e-5

OFF_F = 0
OFF_U = OFF_F + FOURIER_WIDTH
OFF_V = OFF_U + SGU_WIDTH
OFF_Q = OFF_V + SGU_WIDTH
OFF_K = OFF_Q + DIFF_QK_WIDTH
OFF_VA = OFF_K + DIFF_QK_WIDTH
OFF_G = OFF_VA + DIFF_V_WIDTH
IN_WIDTH = OFF_G + N_BRANCHES * D_MODEL

TOP_K = 2
LANES = 128
SUBLANES = 8
MXU_WIDTH = 256
MOE_ROWS = 512
DFT_ROW_SPLIT = 64
SCORE_LEAD = 2
D_FF_PAD = 2816
ROUTER_PAD = LANES
MIB = 1024 * 1024


def _tiles(S):
    t = dict(
        tm_in=min(512, S),
        tm_comb=min(256, S),
        tm_dft=min(512, S),
        tq=min(512, S),
        tkc=min(128, S),
        tkp=min(512, S),
        tf=D_FF_PAD // 2,
    )
    return t


def _layer_norm(z, g, b):
    mu = jnp.mean(z, axis=-1, keepdims=True)
    zc = z - mu
    var = jnp.mean(zc * zc, axis=-1, keepdims=True)
    return zc * lax.rsqrt(var + LN_EPS) * g + b


def _dot(a, b):
    return jnp.dot(a, b, preferred_element_type=F32)


def _inproj_kernel(x_ref, w_ref, dft_ref, vng_ref, vnb_ref, sguw_ref, sgub_ref, cos_ref, sin_ref,
                   g_out, so_out, qt_out, k_out, vat_out):
    tm = x_ref.shape[0]
    xb = x_ref[...].astype(BF16)

    def proj(a, b):
        return _dot(xb, w_ref[:, a:b])

    f = proj(OFF_F, OFF_U)
    g_out[...] = _dot(f.astype(BF16), dft_ref[...]).astype(BF16)

    u = proj(OFF_U, OFF_V)
    v = proj(OFF_V, OFF_Q)
    vn = _layer_norm(v, vng_ref[...], vnb_ref[...]).astype(BF16)
    row_head = lax.broadcasted_iota(jnp.int32, (SGU_HEADS * CHUNK, SGU_WIDTH), 0) // CHUNK
    col_head = lax.broadcasted_iota(jnp.int32, (SGU_HEADS * CHUNK, SGU_WIDTH), 1) // SGU_HEAD_DIM
    head_mask = row_head == col_head
    for c in range(tm // CHUNK):
        rows = slice(c * CHUNK, (c + 1) * CHUNK)
        vb = vn[rows, :]
        rhs = jnp.where(head_mask, jnp.concatenate([vb] * SGU_HEADS, axis=0), jnp.zeros((), BF16))
        mixed = _dot(sguw_ref[...], rhs) + sgub_ref[...]
        so_out[rows, :] = (u[rows, :] * mixed).astype(BF16)

    cs = cos_ref[...]
    sn = sin_ref[...]
    lane = lax.broadcasted_iota(jnp.int32, (tm, LANES), 1)
    first_half = (lane % DIFF_HEAD_DIM) < (DIFF_HEAD_DIM // 2)

    def rope(t):
        outs = []
        for j in range(DIFF_QK_WIDTH // LANES):
            tc = t[:, j * LANES:(j + 1) * LANES]
            rot = jnp.where(first_half,
                            pltpu.roll(tc, LANES - DIFF_HEAD_DIM // 2, 1),
                            pltpu.roll(tc, DIFF_HEAD_DIM // 2, 1))
            outs.append(tc * cs + rot * sn)
        return jnp.concatenate(outs, axis=1)

    q = rope(proj(OFF_Q, OFF_K)) * (DIFF_HEAD_DIM ** -0.5)
    qt_out[...] = q.T.astype(BF16)
    k_out[...] = rope(proj(OFF_K, OFF_VA)).astype(BF16)
    vat_out[...] = proj(OFF_VA, OFF_G).T.astype(BF16)


def _inproj(x, w_a, dft64, vn_g, vn_b, sgu_wcat, sgu_bias, rope_cos, rope_sin, S):
    T = x.shape[0]
    tm = _tiles(S)["tm_in"]
    n_pos = S // tm
    full = lambda shape: pl.BlockSpec(shape, lambda i: (0,) * len(shape))
    return pl.pallas_call(
        _inproj_kernel,
        grid=(T // tm,),
        in_specs=[
            pl.BlockSpec((tm, D_MODEL), lambda i: (i, 0)),
            full((D_MODEL, OFF_G)),
            full((FOURIER_WIDTH, 2 * FOURIER_WIDTH)),
            full((1, SGU_WIDTH)),
            full((1, SGU_WIDTH)),
            full((CHUNK, SGU_HEADS * CHUNK)),
            full((CHUNK, SGU_WIDTH)),
            pl.BlockSpec((tm, LANES), lambda i: (i % n_pos, 0)),
            pl.BlockSpec((tm, LANES), lambda i: (i % n_pos, 0)),
        ],
        out_specs=[
            pl.BlockSpec((tm, 2 * FOURIER_WIDTH), lambda i: (i, 0)),
            pl.BlockSpec((tm, SGU_WIDTH), lambda i: (i, 0)),
            pl.BlockSpec((DIFF_QK_WIDTH, tm), lambda i: (0, i)),
            pl.BlockSpec((tm, DIFF_QK_WIDTH), lambda i: (i, 0)),
            pl.BlockSpec((DIFF_V_WIDTH, tm), lambda i: (0, i)),
        ],
        out_shape=[
            jax.ShapeDtypeStruct((T, 2 * FOURIER_WIDTH), BF16),
            jax.ShapeDtypeStruct((T, SGU_WIDTH), BF16),
            jax.ShapeDtypeStruct((DIFF_QK_WIDTH, T), BF16),
            jax.ShapeDtypeStruct((T, DIFF_QK_WIDTH), BF16),
            jax.ShapeDtypeStruct((DIFF_V_WIDTH, T), BF16),
        ],
        compiler_params=pltpu.CompilerParams(
            dimension_semantics=("parallel",), vmem_limit_bytes=48 * MIB),
        name="inproj",
    )(x, w_a, dft64, vn_g, vn_b, sgu_wcat, sgu_bias, rope_cos, rope_sin)


def _fourier_kernel(ct_ref, snt_ref, g_ref, o_ref):
    acc = _dot(ct_ref[...], g_ref[:, :FOURIER_WIDTH])
    acc = acc + _dot(snt_ref[...], g_ref[:, FOURIER_WIDTH:])
    o_ref[...] = acc.astype(BF16)


def _fourier(g, ct, snt, B, S):
    tm = _tiles(S)["tm_dft"]
    n_row = S // tm
    return pl.pallas_call(
        _fourier_kernel,
        grid=(n_row, B),
        in_specs=[
            pl.BlockSpec((tm, S), lambda i, b: (i, 0), pipeline_mode=pl.Buffered(1)),
            pl.BlockSpec((tm, S), lambda i, b: (i, 0), pipeline_mode=pl.Buffered(1)),
            pl.BlockSpec((S, 2 * FOURIER_WIDTH), lambda i, b: (b, 0)),
        ],
        out_specs=pl.BlockSpec((tm, FOURIER_WIDTH), lambda i, b: (b * n_row + i, 0)),
        out_shape=jax.ShapeDtypeStruct((B * S, FOURIER_WIDTH), BF16),
        compiler_params=pltpu.CompilerParams(
            dimension_semantics=("parallel", "parallel"), vmem_limit_bytes=48 * MIB),
        name="fourier",
    )(ct, snt, g)


def _attn_kernel(lam_ref, qt_ref, k_ref, vt_ref, g_ref, o_ref, s_ref, e_ref, m_ref, *, tkp, lambda_init):
    tq = qt_ref.shape[1]
    width = 2 * tq
    S = k_ref.shape[0]
    n_p = S // tkp
    groups = tkp // SUBLANES
    ring = s_ref.shape[0]
    lp = lam_ref[...]
    lam = (jnp.exp(jnp.sum(lp[0:1, :] * lp[1:2, :], axis=-1, keepdims=True))
           - jnp.exp(jnp.sum(lp[2:3, :] * lp[3:4, :], axis=-1, keepdims=True))
           + lambda_init)
    qt = qt_ref[...]
    row = lax.broadcasted_iota(jnp.int32, qt.shape, 0)
    zero = jnp.zeros((), BF16)
    qt_both = jnp.concatenate(
        [jnp.where(row < DIFF_HEAD_DIM, qt, zero), jnp.where(row >= DIFF_HEAD_DIM, qt, zero)], axis=1)

    m_class = jnp.full((SUBLANES, width), -jnp.inf, F32)
    m_exp = m_class
    l_class = jnp.zeros((SUBLANES, width), F32)
    l_hist = []
    for p in range(n_p + 1):
        if p < n_p:
            hold = 0
            if p >= ring:
                bits = pltpu.bitcast(l_hist[p - ring], jnp.int32)[0, 0]
                half = jnp.int32(16)
                hold = lax.shift_right_logical(lax.shift_right_logical(bits, half), half)
            kc = k_ref[pl.ds(pl.multiple_of(p * tkp + hold, tkp), tkp), :]
            s = _dot(kc, qt_both)
            s_ref[p % ring] = s
            m_class = jnp.maximum(m_class, jnp.max(s.reshape(groups, SUBLANES, width), axis=0))
            m_ref[p] = m_class
        if p >= 1:
            j = p - 1
            m_new = m_ref[j]
            e = jnp.exp(s_ref[j % ring].reshape(groups, SUBLANES, width) - m_new[None])
            e_ref[j * tkp:(j + 1) * tkp, :] = e.reshape(tkp, width).astype(BF16)
            l_class = l_class * jnp.exp(m_exp - m_new) + jnp.sum(e, axis=0)
            m_exp = m_new
            l_hist.append(l_class)

    m_all = jnp.max(m_exp, axis=0, keepdims=True)
    l_all = jnp.sum(l_class * jnp.exp(m_exp - m_all), axis=0, keepdims=True)
    scale = jnp.concatenate([1.0 / l_all[:, :tq], lam / l_all[:, tq:]], axis=1)
    packed = 2 * SUBLANES
    acc = jnp.zeros((DIFF_V_DIM, tq), F32)
    for p in range(n_p):
        f = jnp.exp(m_ref[p] - m_all) * scale
        f = jnp.concatenate([f, f], axis=0).astype(BF16)
        ew = e_ref[p * tkp:(p + 1) * tkp, :].reshape(tkp // packed, packed, width) * f[None]
        a = (ew[:, :, :tq] - ew[:, :, tq:]).reshape(tkp, tq)
        acc = acc + _dot(vt_ref[:, p * tkp:(p + 1) * tkp], a)

    o = acc.T
    o = o * lax.rsqrt(jnp.mean(o * o, axis=-1, keepdims=True) + RMS_EPS) * g_ref[...]
    o_ref[...] = (o * (1.0 - lambda_init)).astype(BF16)


def _attn(lam_params, qt, k, vat, subln_g, B, S, lambda_init):
    t = _tiles(S)
    tq, tkp = t["tq"], t["tkp"]
    n_q = S // tq
    kern = functools.partial(_attn_kernel, tkp=tkp, lambda_init=lambda_init)
    return pl.pallas_call(
        kern,
        grid=(B, DIFF_HEADS, n_q),
        in_specs=[
            pl.BlockSpec((4, DIFF_HEAD_DIM), lambda b, h, i: (0, 0)),
            pl.BlockSpec((DIFF_V_DIM, tq), lambda b, h, i: (h, b * n_q + i)),
            pl.BlockSpec((S, DIFF_V_DIM), lambda b, h, i: (b, h)),
            pl.BlockSpec((DIFF_V_DIM, S), lambda b, h, i: (h, b)),
            pl.BlockSpec((1, DIFF_V_DIM), lambda b, h, i: (0, 0)),
        ],
        out_specs=pl.BlockSpec((tq, DIFF_V_DIM), lambda b, h, i: (b * n_q + i, h)),
        out_shape=jax.ShapeDtypeStruct((B * S, DIFF_V_WIDTH), BF16),
        scratch_shapes=[pltpu.VMEM((min(SCORE_LEAD, S // tkp), tkp, 2 * tq), F32),
                        pltpu.VMEM((S, 2 * tq), BF16),
                        pltpu.VMEM((S // tkp, SUBLANES, 2 * tq), F32)],
        compiler_params=pltpu.CompilerParams(
            dimension_semantics=("parallel", "parallel", "parallel"), vmem_limit_bytes=48 * MIB),
        name="attn",
    )(lam_params, qt, k, vat, subln_g)


def _merge_kernel(*refs, with_router):
    if with_router:
        (x_ref, fo_ref, so_ref, do_ref, wg_ref, wf_ref, ws_ref, wd_ref, wo_ref, lg_ref, lb_ref,
         wr_ref, x1_out, route_out, counts_out, base_ref) = refs
    else:
        (x_ref, fo_ref, so_ref, do_ref, wg_ref, wf_ref, ws_ref, wd_ref, wo_ref, lg_ref, lb_ref,
         x1_out) = refs
    x = x_ref[...]
    xb = x.astype(BF16)
    merged = None
    for j, (br_ref, wb_ref) in enumerate(((fo_ref, wf_ref), (so_ref, ws_ref), (do_ref, wd_ref))):
        gate = jax.nn.sigmoid(_dot(xb, wg_ref[:, j * D_MODEL:(j + 1) * D_MODEL]))
        term = gate * _dot(br_ref[...], wb_ref[...])
        merged = term if merged is None else merged + term
    y = _dot(merged.astype(BF16), wo_ref[...])
    x1 = _layer_norm(ALPHA * x + y, lg_ref[...], lb_ref[...])
    x1_out[...] = x1

    if with_router:
        tm = x.shape[0]
        logits = _dot(x1.astype(BF16), wr_ref[...])
        lane = lax.broadcasted_iota(jnp.int32, (tm, ROUTER_PAD), 1)
        neg = jnp.float32(-jnp.inf)
        lg = jnp.where(lane < N_EXPERTS, logits, neg)
        v1 = jnp.max(lg, axis=-1, keepdims=True)
        i1 = jnp.min(jnp.where(lg == v1, lane, ROUTER_PAD), axis=-1, keepdims=True)
        lg2 = jnp.where(lane == i1, neg, lg)
        v2 = jnp.max(lg2, axis=-1, keepdims=True)
        i2 = jnp.min(jnp.where(lg2 == v2, lane, ROUTER_PAD), axis=-1, keepdims=True)
        e2 = jnp.exp(v2 - v1)
        den = 1.0 + e2

        @pl.when(pl.program_id(0) == 0)
        def _():
            base_ref[...] = jnp.zeros_like(base_ref)

        pick1 = lane == i1
        pick2 = lane == i2
        sel = jnp.logical_or(pick1, pick2)
        r_id = lax.broadcasted_iota(jnp.int32, (tm, tm), 0)
        c_id = lax.broadcasted_iota(jnp.int32, (tm, tm), 1)
        ltri = (c_id < r_id).astype(BF16)
        base = base_ref[0:1, :]
        rank = base + _dot(ltri, sel.astype(BF16))
        r1 = jnp.sum(jnp.where(pick1, rank, 0.0), axis=-1, keepdims=True)
        r2 = jnp.sum(jnp.where(pick2, rank, 0.0), axis=-1, keepdims=True)
        new_base = base + jnp.sum(sel.astype(F32), axis=0, keepdims=True)
        base_ref[...] = jnp.broadcast_to(new_base, base_ref.shape)
        counts_out[...] = jnp.broadcast_to(new_base, counts_out.shape)
        fields = (1.0 / den, e2 / den, i1.astype(F32), i2.astype(F32), r1, r2)
        route = jnp.zeros((tm, ROUTER_PAD), F32)
        for n, val in enumerate(fields):
            route = jnp.where(lane == n, val, route)
        route_out[...] = route


def _merge(x, fo, so, do, w_g, w_f, w_s, w_d, w_o, ln_g, ln_b, w_router, S):
    T = x.shape[0]
    tm = _tiles(S)["tm_in"]
    with_router = w_router is not None
    full = lambda shape: pl.BlockSpec(shape, lambda i: (0,) * len(shape))
    row = lambda width: pl.BlockSpec((tm, width), lambda i: (i, 0))
    in_specs = [
        row(D_MODEL), row(FOURIER_WIDTH), row(SGU_WIDTH), row(DIFF_V_WIDTH),
        full((D_MODEL, N_BRANCHES * D_MODEL)),
        full((FOURIER_WIDTH, D_MODEL)), full((SGU_WIDTH, D_MODEL)), full((DIFF_V_WIDTH, D_MODEL)),
        full((D_MODEL, D_MODEL)), full((1, D_MODEL)), full((1, D_MODEL)),
    ]
    args = [x, fo, so, do, w_g, w_f, w_s, w_d, w_o, ln_g, ln_b]
    out_specs = [row(D_MODEL)]
    out_shape = [jax.ShapeDtypeStruct((T, D_MODEL), F32)]
    scratch = []
    if with_router:
        in_specs.append(full((D_MODEL, ROUTER_PAD)))
        args.append(w_router)
        out_specs += [row(ROUTER_PAD), full((SUBLANES, ROUTER_PAD))]
        out_shape += [jax.ShapeDtypeStruct((T, ROUTER_PAD), F32),
                      jax.ShapeDtypeStruct((SUBLANES, ROUTER_PAD), F32)]
        scratch.append(pltpu.VMEM((SUBLANES, ROUTER_PAD), F32))
    outs = pl.pallas_call(
        functools.partial(_merge_kernel, with_router=with_router),
        grid=(T // tm,),
        in_specs=in_specs,
        out_specs=out_specs,
        out_shape=out_shape,
        scratch_shapes=scratch,
        compiler_params=pltpu.CompilerParams(
            dimension_semantics=("arbitrary" if with_router else "parallel",),
            vmem_limit_bytes=56 * MIB),
        name="merge_router" if with_router else "merge",
    )(*args)
    return (outs[0], outs[1], outs[2]) if with_router else (outs[0], None, None)


def _swiglu(xb, wg_ref, wu_ref, wd_ref, tf):
    y = None
    for f in range(D_FF_PAD // tf):
        cols = slice(f * tf, (f + 1) * tf)
        h = jax.nn.silu(_dot(xb, wg_ref[:, cols])) * _dot(xb, wu_ref[:, cols])
        part = _dot(h.astype(BF16), wd_ref[cols, :])
        y = part if y is None else y + part
    return y


def _ffn_kernel(x_ref, wg_ref, wu_ref, wd_ref, lg_ref, lb_ref, o_ref, *, tf):
    x = x_ref[...]
    ff = _swiglu(x.astype(BF16), wg_ref, wu_ref, wd_ref, tf)
    o_ref[...] = _layer_norm(ALPHA * x + ff, lg_ref[...], lb_ref[...])


def _ffn(x, w_gate, w_up, w_down, ln_g, ln_b, S):
    T = x.shape[0]
    t = _tiles(S)
    tm, tf = t["tm_in"], t["tf"]
    resident = lambda shape: pl.BlockSpec(shape, lambda i: (0, 0), pipeline_mode=pl.Buffered(1))
    return pl.pallas_call(
        functools.partial(_ffn_kernel, tf=tf),
        grid=(T // tm,),
        in_specs=[
            pl.BlockSpec((tm, D_MODEL), lambda i: (i, 0)),
            resident((D_MODEL, D_FF_PAD)), resident((D_MODEL, D_FF_PAD)), resident((D_FF_PAD, D_MODEL)),
            pl.BlockSpec((1, D_MODEL), lambda i: (0, 0)),
            pl.BlockSpec((1, D_MODEL), lambda i: (0, 0)),
        ],
        out_specs=pl.BlockSpec((tm, D_MODEL), lambda i: (i, 0)),
        out_shape=jax.ShapeDtypeStruct((T, D_MODEL), F32),
        compiler_params=pltpu.CompilerParams(
            dimension_semantics=("parallel",), vmem_limit_bytes=56 * MIB),
        name="ffn_dense",
    )(x, w_gate, w_up, w_down, ln_g, ln_b)


def _route_tables(route, counts, T):
    R = MOE_ROWS
    as_int = lambda n: route[:, n].astype(jnp.int32)
    i1, i2, r1, r2 = as_int(2), as_int(3), as_int(4), as_int(5)
    cnt = counts[0, :N_EXPERTS].astype(jnp.int32)
    padded = (cnt + R - 1) // R * R
    ends = jnp.cumsum(padded)
    off = ends - padded
    n_tiles = TOP_K * T // R + N_EXPERTS
    tile_expert = jnp.minimum(
        jnp.searchsorted(ends, jnp.arange(n_tiles, dtype=jnp.int32) * R, side="right"),
        N_EXPERTS - 1).astype(jnp.int32)
    pad_start = off + cnt
    pad_n = (padded - cnt).at[N_EXPERTS - 1].set(n_tiles * R - pad_start[N_EXPERTS - 1])
    return dict(
        pos1=jnp.take(off, i1) + r1, pos2=jnp.take(off, i2) + r2,
        tile_expert=tile_expert, n_used=(ends[-1:] // R).astype(jnp.int32),
        pad=jnp.concatenate([pad_start, pad_n]).astype(jnp.int32), n_tiles=n_tiles)


def _dispatch_kernel(pad_ref, x_ref, p1_ref, p2_ref, q1_ref, q2_ref, xs_ref, x3_ref, zero_ref, sems):
    tm = x_ref.shape[0]
    step = pl.program_id(0)
    n_steps = pl.num_programs(0)
    pad_sem = 2

    def row_copy(src_ref, dst_row, sem_index):
        return pltpu.make_async_copy(src_ref, xs_ref.at[dst_row], sems.at[sem_index])

    @pl.when(step == 0)
    def _():
        zero_ref[...] = jnp.zeros_like(zero_ref)
        for e in range(N_EXPERTS):
            start = pad_ref[e]
            n = pad_ref[N_EXPERTS + e]

            def issue_pad(k, c):
                row_copy(zero_ref.at[0], start + k, pad_sem).start()
                return c

            def drain_pad(k, c):
                row_copy(zero_ref.at[0], start + k, pad_sem).wait()
                return c

            lax.fori_loop(0, n, issue_pad, 0)
            lax.fori_loop(0, n, drain_pad, 0)

    def tile_copies(pa_ref, pb_ref, slot, t):
        return (row_copy(x3_ref.at[slot, t], pa_ref[0, 0, t], slot),
                row_copy(x3_ref.at[slot, t], pb_ref[0, 0, t], slot))

    def issue_tile(pa_ref, pb_ref, slot):
        def issue(t, c):
            for priority, cp in enumerate(tile_copies(pa_ref, pb_ref, slot, t)):
                cp.start(priority=priority)
            return c
        lax.fori_loop(0, tm, issue, 0, unroll=8)

    def drain_tile(pa_ref, pb_ref, slot):
        def drain(t, c):
            for cp in tile_copies(pa_ref, pb_ref, slot, t):
                cp.wait()
            return c
        lax.fori_loop(0, tm, drain, 0, unroll=8)

    for slot in range(2):
        @pl.when(step % 2 == slot)
        def _():
            x3_ref[slot] = x_ref[...].reshape(tm, SUBLANES, LANES)
            issue_tile(p1_ref, p2_ref, slot)

            @pl.when(step > 0)
            def _():
                drain_tile(q1_ref, q2_ref, 1 - slot)

            @pl.when(step == n_steps - 1)
            def _():
                drain_tile(p1_ref, p2_ref, slot)


def _dispatch(x, tabs, S):
    T = x.shape[0]
    tm = _tiles(S)["tm_in"]
    n_rows = tabs["n_tiles"] * MOE_ROWS
    idx_spec = pl.BlockSpec((1, 1, tm), lambda i, pad: (i, 0, 0), memory_space=pltpu.SMEM)
    prev_spec = pl.BlockSpec((1, 1, tm), lambda i, pad: (jnp.maximum(i - 1, 0), 0, 0),
                             memory_space=pltpu.SMEM)
    pos1 = tabs["pos1"].reshape(T // tm, 1, tm)
    pos2 = tabs["pos2"].reshape(T // tm, 1, tm)
    return pl.pallas_call(
        _dispatch_kernel,
        grid_spec=pltpu.PrefetchScalarGridSpec(
            num_scalar_prefetch=1,
            grid=(T // tm,),
            in_specs=[pl.BlockSpec((tm, D_MODEL), lambda i, pad: (i, 0)),
                      idx_spec, idx_spec, prev_spec, prev_spec],
            out_specs=pl.BlockSpec(memory_space=pl.ANY),
            scratch_shapes=[pltpu.VMEM((2, tm, SUBLANES, LANES), F32), pltpu.VMEM((1, SUBLANES, LANES), F32),
                            pltpu.SemaphoreType.DMA((3,))],
        ),
        out_shape=jax.ShapeDtypeStruct((n_rows, SUBLANES, LANES), F32),
        compiler_params=pltpu.CompilerParams(dimension_semantics=("arbitrary",)),
        name="moe_dispatch",
    )(tabs["pad"], x, pos1, pos2, pos1, pos2)


def _gmm_kernel(te_ref, nu_ref, x_ref, wg_ref, wu_ref, wd_ref, o_ref, *, tf):
    used = pl.program_id(0) < nu_ref[0]

    @pl.when(jnp.logical_not(used))
    def _():
        o_ref[...] = jnp.zeros_like(o_ref)

    @pl.when(used)
    def _():
        x = x_ref[...].reshape(x_ref.shape[0], D_MODEL)
        y = _swiglu(x.astype(BF16), wg_ref.at[0], wu_ref.at[0], wd_ref.at[0], tf)
        o_ref[...] = y.reshape(o_ref.shape)


def _gmm(xs, tabs, w_gate, w_up, w_down, S):
    R = MOE_ROWS
    tf = _tiles(S)["tf"]
    row_spec = pl.BlockSpec((R, SUBLANES, LANES), lambda j, te, nu: (j, 0, 0))
    w_spec = lambda shape: pl.BlockSpec(shape, lambda j, te, nu: (te[j], 0, 0),
                                        pipeline_mode=pl.Buffered(1))
    return pl.pallas_call(
        functools.partial(_gmm_kernel, tf=tf),
        grid_spec=pltpu.PrefetchScalarGridSpec(
            num_scalar_prefetch=2,
            grid=(tabs["n_tiles"],),
            in_specs=[row_spec, w_spec((1, D_MODEL, D_FF_PAD)), w_spec((1, D_MODEL, D_FF_PAD)),
                      w_spec((1, D_FF_PAD, D_MODEL))],
            out_specs=row_spec,
        ),
        out_shape=jax.ShapeDtypeStruct(xs.shape, F32),
        compiler_params=pltpu.CompilerParams(
            dimension_semantics=("arbitrary",), vmem_limit_bytes=56 * MIB),
        name="moe_gmm",
    )(tabs["tile_expert"], tabs["n_used"], xs, w_gate, w_up, w_down)


def _combine_kernel(x_ref, route_ref, p1_ref, p2_ref, n1_ref, n2_ref, lg_ref, lb_ref, ys_ref, o_ref,
                    y1_ref, y2_ref, sems):
    tm = x_ref.shape[0]
    step = pl.program_id(0)
    n_steps = pl.num_programs(0)

    def row_copies(pa_ref, pb_ref, slot, t):
        return (pltpu.make_async_copy(ys_ref.at[pa_ref[0, 0, t]], y1_ref.at[slot, t], sems.at[slot]),
                pltpu.make_async_copy(ys_ref.at[pb_ref[0, 0, t]], y2_ref.at[slot, t], sems.at[slot]))

    def issue_tile(pa_ref, pb_ref, slot):
        def issue(t, c):
            for priority, cp in enumerate(row_copies(pa_ref, pb_ref, slot, t)):
                cp.start(priority=priority)
            return c
        lax.fori_loop(0, tm, issue, 0, unroll=8)

    def drain_tile(pa_ref, pb_ref, slot):
        def drain(t, c):
            for cp in row_copies(pa_ref, pb_ref, slot, t):
                cp.wait()
            return c
        lax.fori_loop(0, tm, drain, 0, unroll=8)

    @pl.when(step == 0)
    def _():
        issue_tile(p1_ref, p2_ref, 0)

    for slot in range(2):
        @pl.when(step % 2 == slot)
        def _():
            @pl.when(step + 1 < n_steps)
            def _():
                issue_tile(n1_ref, n2_ref, 1 - slot)

            drain_tile(p1_ref, p2_ref, slot)
            route = route_ref[...]
            ff = (route[:, 0:1] * y1_ref[slot].reshape(tm, D_MODEL)
                  + route[:, 1:2] * y2_ref[slot].reshape(tm, D_MODEL))
            o_ref[...] = _layer_norm(ALPHA * x_ref[...] + ff, lg_ref[...], lb_ref[...])


def _combine(x, route, tabs, ys, ln_g, ln_b, S):
    T = x.shape[0]
    tm = _tiles(S)["tm_comb"]
    n_steps = T // tm
    idx_spec = pl.BlockSpec((1, 1, tm), lambda i: (i, 0, 0), memory_space=pltpu.SMEM)
    next_spec = pl.BlockSpec((1, 1, tm), lambda i: (jnp.minimum(i + 1, n_steps - 1), 0, 0),
                             memory_space=pltpu.SMEM)
    pos1 = tabs["pos1"].reshape(n_steps, 1, tm)
    pos2 = tabs["pos2"].reshape(n_steps, 1, tm)
    return pl.pallas_call(
        _combine_kernel,
        grid=(n_steps,),
        in_specs=[
            pl.BlockSpec((tm, D_MODEL), lambda i: (i, 0)),
            pl.BlockSpec((tm, ROUTER_PAD), lambda i: (i, 0)),
            idx_spec, idx_spec, next_spec, next_spec,
            pl.BlockSpec((1, D_MODEL), lambda i: (0, 0)),
            pl.BlockSpec((1, D_MODEL), lambda i: (0, 0)),
            pl.BlockSpec(memory_space=pl.ANY),
        ],
        out_specs=pl.BlockSpec((tm, D_MODEL), lambda i: (i, 0)),
        out_shape=jax.ShapeDtypeStruct((T, D_MODEL), F32),
        scratch_shapes=[pltpu.VMEM((2, tm, SUBLANES, LANES), F32), pltpu.VMEM((2, tm, SUBLANES, LANES), F32),
                        pltpu.SemaphoreType.DMA((2,))],
        compiler_params=pltpu.CompilerParams(dimension_semantics=("arbitrary",)),
        name="moe_combine",
    )(x, route, pos1, pos2, pos1, pos2, ln_g, ln_b, ys)


def _rope_tables(S):
    half = DIFF_HEAD_DIM // 2
    inv = ROPE_THETA ** (-jnp.arange(0, DIFF_HEAD_DIM, 2, dtype=F32) / DIFF_HEAD_DIM)
    ang = jnp.arange(S, dtype=F32)[:, None] * inv[None, :]
    cos, sin = jnp.cos(ang), jnp.sin(ang)
    reps = LANES // DIFF_HEAD_DIM
    cos_t = jnp.tile(jnp.concatenate([cos, cos], axis=1), (1, reps))
    sin_t = jnp.tile(jnp.concatenate([-sin, sin], axis=1), (1, reps))
    assert cos_t.shape == (S, LANES) and half * 2 == DIFF_HEAD_DIM
    return cos_t, sin_t


def _dft_tables(S):
    split = math.gcd(S, DFT_ROW_SPLIT)
    k = jnp.arange(S, dtype=jnp.int32)[None, :]

    def thin(rows):
        ang = ((rows[:, None] * k) % S).astype(F32) * (2.0 * math.pi / S)
        return jnp.cos(ang), jnp.sin(ang)

    ch, sh = thin(jnp.arange(S // split, dtype=jnp.int32) * split)
    cl, sl = thin(jnp.arange(split, dtype=jnp.int32))
    scale = S ** -0.5
    cos = (ch[:, None, :] * cl[None, :, :] - sh[:, None, :] * sl[None, :, :]) * scale
    sin = (sh[:, None, :] * cl[None, :, :] + ch[:, None, :] * sl[None, :, :]) * -scale
    return cos.reshape(S, S).astype(BF16), sin.reshape(S, S).astype(BF16)


def _dft64_table():
    j = jnp.arange(FOURIER_GROUP, dtype=jnp.int32)
    ang = ((j[:, None] * j[None, :]) % FOURIER_GROUP).astype(F32) * (2.0 * math.pi / FOURIER_GROUP)
    scale = FOURIER_GROUP ** -0.5
    eye = jnp.eye(N_FOURIER_GROUPS, dtype=F32)
    return jnp.concatenate([jnp.kron(eye, jnp.cos(ang) * scale),
                            jnp.kron(eye, jnp.sin(ang) * scale)], axis=1).astype(BF16)


def _pad_ff(w, axis):
    pad = [(0, 0)] * w.ndim
    pad[axis] = (0, D_FF_PAD - D_FF)
    return jnp.pad(w, pad).astype(BF16)


def _prepare(w_in,w_fourier, w_sgu, w_diff, w_out, vn_g, vn_b, sgu_w, sgu_b,
             lam_q1, lam_k1, lam_q2, lam_k2, subln_g, ln1_g, ln1_b, ln2_g, ln2_b,
             ffn_w_gate, ffn_w_up, ffn_w_down, w_router, moe_w_gate, moe_w_up, moe_w_down):
    w_in_b = w_in.astype(BF16)
    row = lambda a: a.astype(F32)[:, None, :]
    return dict(
        w_a=w_in_b[:, :, :OFF_G], w_g=w_in_b[:, :, OFF_G:],
        w_f=w_fourier.astype(BF16), w_s=w_sgu.astype(BF16), w_d=w_diff.astype(BF16),
        w_o=w_out.astype(BF16),
        vn_g=row(vn_g), vn_b=row(vn_b),
        sgu_wcat=jnp.transpose(sgu_w, (0, 2, 1, 3)).reshape(DEPTH, CHUNK, SGU_HEADS * CHUNK).astype(BF16),
        sgu_bias=jnp.repeat(jnp.transpose(sgu_b, (0, 2, 1)).astype(F32), SGU_HEAD_DIM, axis=2),
        lam=jnp.stack([lam_q1, lam_k1, lam_q2, lam_k2], axis=1).astype(F32),
        subln_g=row(subln_g), ln1_g=row(ln1_g), ln1_b=row(ln1_b), ln2_g=row(ln2_g), ln2_b=row(ln2_b),
        ffn_gate=_pad_ff(ffn_w_gate, 2), ffn_up=_pad_ff(ffn_w_up, 2), ffn_down=_pad_ff(ffn_w_down, 1),
        w_router=jnp.pad(w_router, ((0, 0), (0, 0), (0, ROUTER_PAD - N_EXPERTS))).astype(BF16),
        moe_gate=_pad_ff(moe_w_gate, 3), moe_up=_pad_ff(moe_w_up, 3), moe_down=_pad_ff(moe_w_down, 2),
        dft64=_dft64_table(),
    )


def _trunk(x3, p):
    B, S, _ = x3.shape
    x = x3.reshape(B * S, D_MODEL)
    rope_cos, rope_sin = _rope_tables(S)
    ct, snt = _dft_tables(S)
    for l in range(DEPTH):
        lambda_init = 0.8 - 0.6 * math.exp(-0.3 * l)
        g, so, qt, k, vat = _inproj(x, p["w_a"][l], p["dft64"], p["vn_g"][l], p["vn_b"][l],
                                    p["sgu_wcat"][l], p["sgu_bias"][l], rope_cos, rope_sin, S)
        fo = _fourier(g, ct, snt, B, S)
        do = _attn(p["lam"][l], qt, k, vat, p["subln_g"][l], B, S, lambda_init)
        j = l // 2
        moe = l % 2 == 1
        x1, route, counts = _merge(x, fo, so, do, p["w_g"][l], p["w_f"][l], p["w_s"][l], p["w_d"][l],
                                   p["w_o"][l], p["ln1_g"][l], p["ln1_b"][l],
                                   p["w_router"][j] if moe else None, S)
        if moe:
            tabs = _route_tables(route, counts, B * S)
            xs = _dispatch(x1, tabs, S)
            ys = _gmm(xs, tabs, p["moe_gate"][j], p["moe_up"][j], p["moe_down"][j], S)
            x = _combine(x1, route, tabs, ys, p["ln2_g"][l], p["ln2_b"][l], S)
        else:
            x = _ffn(x1, p["ffn_gate"][j], p["ffn_up"][j], p["ffn_down"][j],
                     p["ln2_g"][l], p["ln2_b"][l], S)
    return x.reshape(B, S, D_MODEL)


def kernel(x_prompt, x_sample, w_in, w_fourier, w_sgu, w_diff, w_out, vn_g, vn_b, sgu_w, sgu_b,
           lam_q1, lam_k1, lam_q2, lam_k2, subln_g, ln1_g, ln1_b, ln2_g, ln2_b,
           ffn_w_gate, ffn_w_up, ffn_w_down, w_router, moe_w_gate, moe_w_up, moe_w_down):
    p = _prepare(w_in, w_fourier, w_sgu, w_diff, w_out, vn_g, vn_b, sgu_w, sgu_b,
                 lam_q1, lam_k1, lam_q2, lam_k2, subln_g, ln1_g, ln1_b, ln2_g, ln2_b,
                 ffn_w_gate, ffn_w_up, ffn_w_down, w_router, moe_w_gate, moe_w_up, moe_w_down)
    return (_trunk(x_prompt, p), _trunk(x_sample, p))
```
